```python
import jax, jax.numpy as jnp
from jax import lax
import numpy as np

D_MODEL = 1024
BATCH = 4
SEQ = 8192
DEPTH = 1
DEC_BATCH = 128
DEC_SEQ = 1
PAST_LEN = 16384
PAGE_SIZE = 128

HEAD_DIM = 64
A_WINDOWS = (128, 512, 2048)
A_DILATIONS = (1, 4, 16)
A_GROUPS = 3
A_HEADS = 8
A_WIDTH = A_HEADS * HEAD_DIM
B_WINDOW = 128
B_HEADS = 8
B_KV_HEADS = 2
B_WIDTH = B_HEADS * HEAD_DIM
B_KV_WIDTH = B_KV_HEADS * HEAD_DIM
BLOCK = 128
NORM_EPS = 1e-6
ALIBI_MAX_EXP = 8.0
IN_SPLITS = (A_WIDTH,) * (3 * A_GROUPS) + (B_WIDTH, B_KV_WIDTH, B_KV_WIDTH, A_WIDTH, B_WIDTH, D_MODEL, D_MODEL)
IN_WIDTH = sum(IN_SPLITS)
SPLIT_POINTS = tuple(int(c) for c in np.cumsum(IN_SPLITS)[:-1])

kernel_name = "dilated_swa_gated_hybrid_step"


def rms_norm(x, gain):
    xf = x.astype(jnp.float32)
    y = xf * lax.rsqrt(jnp.mean(xf * xf, axis=-1, keepdims=True) + NORM_EPS)
    return (y * gain.astype(jnp.float32)).astype(x.dtype)


def alibi_slopes(n_heads, dilation):
    expo = -ALIBI_MAX_EXP * np.arange(1, n_heads + 1, dtype=np.float32) / n_heads
    return (np.exp2(expo) / dilation).astype(np.float32)


def project_inputs(x, norm_gain, w_in, qk_norm_a, qk_norm_b):
    n, t = x.shape[:2]
    h = rms_norm(x, norm_gain)
    parts = jnp.split(jnp.einsum("ntd,de->nte", h, w_in), SPLIT_POINTS, axis=-1)
    heads = lambda a: a.reshape(n, t, -1, HEAD_DIM)
    a_qkv = [(rms_norm(heads(parts[3 * g]), qk_norm_a[g, 0]),
              rms_norm(heads(parts[3 * g + 1]), qk_norm_a[g, 1]),
              heads(parts[3 * g + 2])) for g in range(A_GROUPS)]
    i = 3 * A_GROUPS
    b_qkv = (rms_norm(heads(parts[i]), qk_norm_b[0]),
             rms_norm(heads(parts[i + 1]), qk_norm_b[1]),
             heads(parts[i + 2]))
    gates = tuple(parts[i + 3:])
    return a_qkv, b_qkv, gates


def masked_softmax(s, valid, sink):
    s = jnp.where(valid, s, -jnp.inf)
    lse = jax.nn.logsumexp(s, axis=-1)
    if sink is not None:
        lse = jnp.logaddexp(lse, sink)
    return jnp.exp(s - lse[..., None]), lse


def banded_attention(q, k, v, slopes, sinks, max_delta, dist):
    n, length, n_heads, hd = q.shape
    kh = k.shape[2]
    rep = n_heads // kh
    nb = -(-length // BLOCK)
    pad = nb * BLOCK - length
    blocks = lambda a: jnp.pad(a, ((0, 0), (0, pad), (0, 0), (0, 0))).reshape(n, nb, BLOCK, a.shape[2], hd)
    shift = lambda a: jnp.pad(a, ((0, 0), (1, 0), (0, 0), (0, 0), (0, 0)))[:, :-1]
    qb = blocks(q).reshape(n, nb, BLOCK, kh, rep, hd)
    kb, vb = blocks(k), blocks(v)
    kk = jnp.concatenate([shift(kb), kb], axis=2)
    vv = jnp.concatenate([shift(vb), vb], axis=2)
    s = jnp.einsum("nbqgrd,nbkgd->nbgrqk", qb, kk).astype(jnp.float32) * (hd ** -0.5)
    delta = np.arange(BLOCK)[:, None] + BLOCK - np.arange(2 * BLOCK)[None, :]
    k_abs = np.arange(nb)[:, None] * BLOCK - BLOCK + np.arange(2 * BLOCK)[None, :]
    valid = (delta >= 0)[None] & (delta <= max_delta)[None] & (k_abs[:, None, :] >= 0)
    bias = -jnp.asarray(slopes).reshape(kh, rep)[:, :, None, None] * (delta * dist).astype(np.float32)
    sink = None if sinks is None else sinks.astype(jnp.float32).reshape(kh, rep)[:, :, None]
    p, lse = masked_softmax(s + bias, valid[:, None, None], sink)
    o = jnp.einsum("nbgrqk,nbkgd->nbqgrd", p.astype(v.dtype), vv)
    o = o.reshape(n, nb * BLOCK, n_heads, hd)[:, :length]
    lse = lse.transpose(0, 1, 4, 2, 3).reshape(n, nb * BLOCK, n_heads)[:, :length]
    return o, lse


def dilated_prompt(q, k, v, window, dil):
    n, s_len = q.shape[:2]
    stride = lambda a: a.reshape(n, s_len // dil, dil, *a.shape[2:]).swapaxes(1, 2).reshape(n * dil, s_len // dil, *a.shape[2:])
    unstride = lambda a: a.reshape(n, dil, s_len // dil, *a.shape[2:]).swapaxes(1, 2).reshape(n, s_len, *a.shape[2:])
    o, lse = banded_attention(stride(q), stride(k), stride(v), alibi_slopes(A_HEADS, dil), None, window // dil, dil)
    return unstride(o), unstride(lse)


def decode_attention(q, cache_kv, k_new, v_new, n_keys, dil, slopes, sinks, window):
    n, t_new, n_heads, hd = q.shape
    kh = k_new.shape[2]
    rep = n_heads // kh
    buf_len = cache_kv.shape[1]
    kk = jnp.concatenate([cache_kv[:, :, 0].astype(k_new.dtype), k_new], axis=1)
    vv = jnp.concatenate([cache_kv[:, :, 1].astype(v_new.dtype), v_new], axis=1)
    steps = np.arange(n_keys)
    idx = buf_len + np.arange(t_new)[:, None] - steps[None, :] * dil
    valid = idx >= 0
    idx = np.maximum(idx, 0)
    kg, vg = kk[:, idx], vv[:, idx]
    s = jnp.einsum("ntgrd,ntkgd->ntgrk", q.reshape(n, t_new, kh, rep, hd), kg).astype(jnp.float32) * (hd ** -0.5)
    bias = -jnp.asarray(slopes).reshape(kh, rep)[:, :, None] * (steps * dil).astype(np.float32)
    sink = None if sinks is None else sinks.astype(jnp.float32).reshape(kh, rep)
    p, lse = masked_softmax(s + bias, valid[:, None, None, :], sink)
    o = jnp.einsum("ntgrk,ntkgd->ntgrd", p.astype(vv.dtype), vg).reshape(n, t_new, n_heads, hd)
    keep = min(window, buf_len + t_new)
    new_kv = jnp.stack([kk[:, -keep:], vv[:, -keep:]], axis=2)
    return o, lse.reshape(n, t_new, n_heads), new_kv


def merge_branches(x, o_a_groups, lse_a, o_b, gates, w_branch_a, w_branch_b, w_out):
    n, t = x.shape[:2]
    z_a, z_b, g_a, g_b = gates
    wts = jax.nn.softmax(lse_a, axis=0)
    o_a = jnp.einsum("gnth,gnthd->nthd", wts.astype(o_a_groups.dtype), o_a_groups)
    a = o_a.reshape(n, t, A_WIDTH) * jax.nn.silu(z_a)
    b = o_b.reshape(n, t, B_WIDTH) * jax.nn.silu(z_b)
    mixed = jax.nn.sigmoid(g_a) * jnp.einsum("nte,ed->ntd", a, w_branch_a) \
        + jax.nn.sigmoid(g_b) * jnp.einsum("nte,ed->ntd", b, w_branch_b)
    return x + jnp.einsum("ntd,de->nte", mixed, w_out)


def layer_prompt(x, norm_gain, w_in, qk_norm_a, qk_norm_b, b_sinks, w_branch_a, w_branch_b, w_out):
    a_qkv, (bq, bk, bv), gates = project_inputs(x, norm_gain, w_in, qk_norm_a, qk_norm_b)
    s_len = x.shape[1]
    o_list, lse_list, new_a = [], [], []
    for g in range(A_GROUPS):
        q, k, v = a_qkv[g]
        o, lse = dilated_prompt(q, k, v, A_WINDOWS[g], A_DILATIONS[g])
        o_list.append(o)
        lse_list.append(lse)
        keep = min(A_WINDOWS[g], s_len)
        new_a.append(jnp.stack([k[:, s_len - keep:], v[:, s_len - keep:]], axis=2))
    o_b, _ = banded_attention(bq, bk, bv, alibi_slopes(B_HEADS, 1), b_sinks, B_WINDOW, 1)
    keep = min(B_WINDOW, s_len)
    new_b = jnp.stack([bk[:, s_len - keep:], bv[:, s_len - keep:]], axis=2)
    y = merge_branches(x, jnp.stack(o_list), jnp.stack(lse_list), o_b, gates, w_branch_a, w_branch_b, w_out)
    return y, new_a, new_b


def layer_sample(x, caches_a, cache_b, norm_gain, w_in, qk_norm_a, qk_norm_b, b_sinks, w_branch_a, w_branch_b, w_out):
    a_qkv, (bq, bk, bv), gates = project_inputs(x, norm_gain, w_in, qk_norm_a, qk_norm_b)
    o_list, lse_list, new_a = [], [], []
    for g in range(A_GROUPS):
        q, k, v = a_qkv[g]
        dil = A_DILATIONS[g]
        o, lse, kv = decode_attention(q, caches_a[g], k, v, A_WINDOWS[g] // dil + 1, dil,
                                      alibi_slopes(A_HEADS, dil), None, A_WINDOWS[g])
        o_list.append(o)
        lse_list.append(lse)
        new_a.append(kv)
    o_b, _, new_b = decode_attention(bq, cache_b, bk, bv, B_WINDOW + 1, 1, alibi_slopes(B_HEADS, 1), b_sinks, B_WINDOW)
    y = merge_branches(x, jnp.stack(o_list), jnp.stack(lse_list), o_b, gates, w_branch_a, w_branch_b, w_out)
    return y, new_a, new_b


def setup_inputs(seed: int = 0) -> dict:
    key = jax.random.key(seed)
    ks = jax.random.split(key, 16)
    nrm = lambda k, shape: jax.random.normal(k, shape, dtype=jnp.float32)
    return {
        "x_prompt": nrm(ks[0], (BATCH, SEQ, D_MODEL)),
        "x_sample": nrm(ks[1], (DEC_BATCH, DEC_SEQ, D_MODEL)),
        "cache_a1_kv": nrm(ks[2], (DEPTH, DEC_BATCH, min(A_WINDOWS[0], PAST_LEN), 2, A_HEADS, HEAD_DIM)),
        "cache_a2_kv": nrm(ks[3], (DEPTH, DEC_BATCH, min(A_WINDOWS[1], PAST_LEN), 2, A_HEADS, HEAD_DIM)),
        "cache_a3_kv": nrm(ks[4], (DEPTH, DEC_BATCH, min(A_WINDOWS[2], PAST_LEN), 2, A_HEADS, HEAD_DIM)),
        "cache_b_kv": nrm(ks[5], (DEPTH, DEC_BATCH, min(B_WINDOW, PAST_LEN), 2, B_KV_HEADS, HEAD_DIM)),
        "norm_gain": 1.0 + 0.02 * nrm(ks[6], (DEPTH, D_MODEL)),
        "w_in": nrm(ks[7], (DEPTH, D_MODEL, IN_WIDTH)) * D_MODEL ** -0.5,
        "qk_norm_a": 1.0 + 0.02 * nrm(ks[8], (DEPTH, A_GROUPS, 2, HEAD_DIM)),
        "qk_norm_b": 1.0 + 0.02 * nrm(ks[9], (DEPTH, 2, HEAD_DIM)),
        "b_sinks": nrm(ks[10], (DEPTH, B_HEADS)),
        "w_branch_a": nrm(ks[11], (DEPTH, A_WIDTH, D_MODEL)) * A_WIDTH ** -0.5,
        "w_branch_b": nrm(ks[12], (DEPTH, B_WIDTH, D_MODEL)) * B_WIDTH ** -0.5,
        "w_out": nrm(ks[13], (DEPTH, D_MODEL, D_MODEL)) * D_MODEL ** -0.5,
    }


def reference(x_prompt, x_sample, cache_a1_kv, cache_a2_kv, cache_a3_kv, cache_b_kv, norm_gain, w_in,
              qk_norm_a, qk_norm_b, b_sinks, w_branch_a, w_branch_b, w_out):
    caches_a = (cache_a1_kv, cache_a2_kv, cache_a3_kv)
    y_prompt, y_sample = x_prompt, x_sample
    pa = [[] for _ in range(A_GROUPS)]
    sa = [[] for _ in range(A_GROUPS)]
    pb, sb = [], []
    for layer in range(DEPTH):
        params = (norm_gain[layer], w_in[layer], qk_norm_a[layer], qk_norm_b[layer], b_sinks[layer],
                  w_branch_a[layer], w_branch_b[layer], w_out[layer])
        y_prompt, new_pa, new_pb = layer_prompt(y_prompt, *params)
        y_sample, new_sa, new_sb = layer_sample(y_sample, [c[layer] for c in caches_a], cache_b_kv[layer], *params)
        for g in range(A_GROUPS):
            pa[g].append(new_pa[g])
            sa[g].append(new_sa[g])
        pb.append(new_pb)
        sb.append(new_sb)
    new_a1_prompt, new_a2_prompt, new_a3_prompt = [jnp.stack(r) for r in pa]
    new_a1_sample, new_a2_sample, new_a3_sample = [jnp.stack(r) for r in sa]
    new_b_prompt = jnp.stack(pb)
    new_b_sample = jnp.stack(sb)
    return (y_prompt, y_sample, new_a1_prompt, new_a2_prompt, new_a3_prompt, new_b_prompt,
            new_a1_sample, new_a2_sample, new_a3_sample, new_b_sample)
```

```python
import functools

import numpy as np
import jax
import jax.numpy as jnp
from jax import lax
from jax.experimental import pallas as pl
from jax.experimental.pallas import tpu as pltpu

F32 = jnp.float32
BF16 = jnp.bfloat16

D_MODEL = 1024
HEAD_DIM = 64
N_HEADS = 8
A_WINDOWS = (128, 512, 2048)
A_DILATIONS = (1, 4, 16)
B_WINDOW = 128
B_KV_HEADS = 2
A_WIDTH = N_HEADS * HEAD_DIM
B_KV_WIDTH = B_KV_HEADS * HEAD_DIM
QKV_WIDTH = 9 * A_WIDTH + A_WIDTH + 2 * B_KV_WIDTH
GATE_WIDTH = 2 * A_WIDTH + 2 * D_MODEL
BLOCK = 128
NORM_EPS = 1e-6
NEG = -1e30
LANES = 128
TOKEN_TILE = 512
VMEM_LIMIT = 56 * 1024 * 1024
B_HEAD_ORDER = (0, 4, 1, 5, 2, 6, 3, 7)

_CHUNKS = []
for _g in range(3):
    _CHUNKS += [(1536 * _g, 512, 2 * _g), (1536 * _g + 512, 512, 2 * _g + 1), (1536 * _g + 1024, 512, None)]
_CHUNKS += [(4608, 512, 6), (5120, 128, 7), (5248, 128, None)]


def _params(**kw):
    return pltpu.CompilerParams(vmem_limit_bytes=VMEM_LIMIT, **kw)


def _resident(shape):
    nd = len(shape)
    return pl.BlockSpec(shape, lambda *_: (0,) * nd, pipeline_mode=pl.Buffered(1))


def _whole(shape):
    nd = len(shape)
    return pl.BlockSpec(shape, lambda *_: (0,) * nd)


def _rms_rows(x, gain_row):
    ms = jnp.mean(x * x, axis=-1, keepdims=True)
    return (x * lax.rsqrt(ms + NORM_EPS)) * gain_row


def _proj_chunk(h, w_ref, bd_ref, gain_ref, chunk):
    off, wd, gain_row = chunk
    acc = jnp.dot(h, w_ref[:, off:off + wd], preferred_element_type=F32)
    if gain_row is None:
        return acc
    ss = jnp.dot((acc * acc).astype(BF16), bd_ref[:wd, :wd], preferred_element_type=F32)
    return (acc * lax.rsqrt(ss * (1.0 / HEAD_DIM) + NORM_EPS)) * gain_ref[gain_row:gain_row + 1, :wd]


def _sigmoid(z):
    return 1.0 / (1.0 + jnp.exp(-z))


def _prompt_proj_body(x_ref, ng_ref, w_ref, bd_ref, gain_ref, *refs, seq, tt):
    outs = refs[:16]
    perm_ref = refs[16]
    t = pl.program_id(1)
    n_tiles = seq // tt
    h = _rms_rows(x_ref[0], ng_ref[...]).astype(BF16)
    for ci, chunk in enumerate(_CHUNKS):
        y = _proj_chunk(h, w_ref, bd_ref, gain_ref, chunk)
        mixer, role = divmod(ci, 3)
        if mixer < 3:
            d = A_DILATIONS[mixer]
            window = A_WINDOWS[mixer]
            o_ref = outs[4 * mixer + role]
            cache_ref = outs[4 * mixer + 3]
            if d == 1:
                o_ref[0, 0] = y.astype(BF16)
            else:
                rows = tt // d
                for j in range(A_WIDTH // LANES):
                    perm_ref[j] = y[:, j * LANES:(j + 1) * LANES]
                for r in range(d):
                    for j in range(A_WIDTH // LANES):
                        o_ref[0, r, :, j * LANES:(j + 1) * LANES] = (
                            perm_ref[j, pl.ds(r, rows, stride=d), :].astype(BF16))
        else:
            window = B_WINDOW
            o_ref = outs[12 + role]
            cache_ref = outs[15]
            o_ref[0] = y.astype(BF16)
        if role > 0:
            if window >= tt:
                first = n_tiles - window // tt

                @pl.when(t >= first)
                def _():
                    cache_ref[0, role - 1] = y.T
            else:
                @pl.when(t == n_tiles - 1)
                def _():
                    cache_ref[0, role - 1] = y[tt - window:, :].T


def _prompt_proj(x, ng, w_qkv, bd, gains):
    nb, seq, _ = x.shape
    tt = TOKEN_TILE
    n_tiles = seq // tt
    out_shape, out_specs = [], []
    for g in range(3):
        d, window = A_DILATIONS[g], A_WINDOWS[g]
        for _ in range(3):
            out_shape.append(jax.ShapeDtypeStruct((nb, d, seq // d, A_WIDTH), BF16))
            out_specs.append(pl.BlockSpec((1, d, tt // d, A_WIDTH), lambda n, t: (n, 0, t, 0)))
        cw = min(window, tt)
        first = n_tiles - max(window // tt, 1)
        out_shape.append(jax.ShapeDtypeStruct((nb, 2, A_WIDTH, window), F32))
        out_specs.append(pl.BlockSpec((1, 2, A_WIDTH, cw),
                                      lambda n, t, first=first: (n, 0, 0, jnp.maximum(t - first, 0))))
    for wd in (A_WIDTH, B_KV_WIDTH, B_KV_WIDTH):
        out_shape.append(jax.ShapeDtypeStruct((nb, seq, wd), BF16))
        out_specs.append(pl.BlockSpec((1, tt, wd), lambda n, t: (n, t, 0)))
    out_shape.append(jax.ShapeDtypeStruct((nb, 2, B_KV_WIDTH, B_WINDOW), F32))
    out_specs.append(pl.BlockSpec((1, 2, B_KV_WIDTH, B_WINDOW), lambda n, t: (n, 0, 0, 0)))
    return pl.pallas_call(
        functools.partial(_prompt_proj_body, seq=seq, tt=tt),
        grid=(nb, n_tiles),
        in_specs=[pl.BlockSpec((1, tt, D_MODEL), lambda n, t: (n, t, 0)),
                  _resident(ng.shape), _resident(w_qkv.shape), _resident(bd.shape), _resident(gains.shape)],
        out_specs=out_specs,
        out_shape=out_shape,
        scratch_shapes=[pltpu.VMEM((A_WIDTH // LANES, tt, LANES), F32)],
        compiler_params=_params(dimension_semantics=("arbitrary", "arbitrary")),
        name="prompt_proj",
    )(x, ng, w_qkv, bd, gains)


def _band_scores(qm, kk, bias):
    s = lax.dot_general(qm, kk, (((1,), (1,)), ((), ())), preferred_element_type=F32)
    return s + bias


def _attn_a_body(q_ref, kp_ref, kc_ref, vp_ref, vc_ref, bias_ref, o_ref, lse_ref):
    sel = jnp.minimum(pl.program_id(1), 1)
    lane = lax.broadcasted_iota(jnp.int32, (BLOCK, LANES), 1)
    lo = lane < HEAD_DIM
    lse_tile = jnp.zeros((BLOCK, LANES), F32)
    for i in range(N_HEADS // 2):
        cols = slice(i * LANES, (i + 1) * LANES)
        qp = q_ref[0, :, cols]
        kk = jnp.concatenate([kp_ref[0, :, cols], kc_ref[0, :, cols]], axis=0)
        vv = jnp.concatenate([vp_ref[0, :, cols], vc_ref[0, :, cols]], axis=0)
        halves = []
        for hh in range(2):
            head = 2 * i + hh
            qm = jnp.where(lo if hh == 0 else ~lo, qp, jnp.zeros_like(qp))
            s = _band_scores(qm, kk, bias_ref[sel, head])
            m = jnp.max(s, axis=-1, keepdims=True)
            p = jnp.exp(s - m)
            l = jnp.sum(p, axis=-1, keepdims=True)
            pv = jnp.dot(p.astype(BF16), vv, preferred_element_type=F32)
            halves.append(pv * (1.0 / l))
            lse_tile = jnp.where(lane == head, m + jnp.log(l), lse_tile)
        o_ref[0, :, cols] = jnp.where(lo, halves[0], halves[1])
    lse_ref[0] = lse_tile


def _attn_a(q, k, v, bias):
    ns, steps, _ = q.shape
    cur = pl.BlockSpec((1, BLOCK, A_WIDTH), lambda s, u: (s, u, 0))
    prev = pl.BlockSpec((1, BLOCK, A_WIDTH), lambda s, u: (s, jnp.maximum(u - 1, 0), 0))
    return pl.pallas_call(
        _attn_a_body,
        grid=(ns, steps // BLOCK),
        in_specs=[cur, prev, cur, prev, cur, _resident(bias.shape)],
        out_specs=[cur, pl.BlockSpec((1, BLOCK, LANES), lambda s, u: (s, u, 0))],
        out_shape=[jax.ShapeDtypeStruct((ns, steps, A_WIDTH), F32),
                   jax.ShapeDtypeStruct((ns, steps, LANES), F32)],
        compiler_params=_params(dimension_semantics=("arbitrary", "arbitrary")),
        name="attn_a",
    )(q, k, k, v, v, bias)


def _attn_b_body(sink_ref, q_ref, kp_ref, kc_ref, vp_ref, vc_ref, bias_ref, o_ref):
    sel = jnp.minimum(pl.program_id(1), 1)
    lane = lax.broadcasted_iota(jnp.int32, (BLOCK, LANES), 1)
    lo = lane < HEAD_DIM
    kk = jnp.concatenate([kp_ref[0], kc_ref[0]], axis=0)
    vv = jnp.concatenate([vp_ref[0], vc_ref[0]], axis=0)
    for c in range(N_HEADS // 2):
        cols = slice(c * LANES, (c + 1) * LANES)
        qp = q_ref[0, :, cols]
        halves = []
        for hh in range(2):
            head = B_HEAD_ORDER[2 * c + hh]
            qm = jnp.where(lo if hh == 0 else ~lo, qp, jnp.zeros_like(qp))
            s = _band_scores(qm, kk, bias_ref[sel, head])
            m = jnp.max(s, axis=-1, keepdims=True)
            p = jnp.exp(s - m)
            l = jnp.sum(p, axis=-1, keepdims=True) + jnp.exp(sink_ref[head] - m)
            pv = jnp.dot(p.astype(BF16), vv, preferred_element_type=F32)
            halves.append(pv * (1.0 / l))
        o_ref[0, :, cols] = jnp.where(lo, halves[0], halves[1])


def _attn_b(sinks, q, k, v, bias):
    nb, seq, _ = q.shape
    kcur = pl.BlockSpec((1, BLOCK, B_KV_WIDTH), lambda n, u: (n, u, 0))
    kprev = pl.BlockSpec((1, BLOCK, B_KV_WIDTH), lambda n, u: (n, jnp.maximum(u - 1, 0), 0))
    qspec = pl.BlockSpec((1, BLOCK, A_WIDTH), lambda n, u: (n, u, 0))
    return pl.pallas_call(
        _attn_b_body,
        grid=(nb, seq // BLOCK),
        in_specs=[pl.BlockSpec(memory_space=pltpu.SMEM), qspec, kprev, kcur, kprev, kcur,
                  _resident(bias.shape)],
        out_specs=qspec,
        out_shape=jax.ShapeDtypeStruct((nb, seq, A_WIDTH), F32),
        compiler_params=_params(dimension_semantics=("arbitrary", "arbitrary")),
        name="attn_b",
    )(sinks, q, k, k, v, v, bias)


def _head_expand(w, e_ref):
    hi = w.astype(BF16)
    lo = (w - hi.astype(F32)).astype(BF16)
    return (jnp.dot(hi, e_ref[...], preferred_element_type=F32)
            + jnp.dot(lo, e_ref[...], preferred_element_type=F32))


def _merge_math(x, ng_ref, wg_ref, pa_ref, pb_ref, wo_ref, e_ref, o_groups, lse_groups, o_b):
    big = jnp.maximum(jnp.maximum(lse_groups[0], lse_groups[1]), lse_groups[2])
    es = [jnp.exp(l - big) for l in lse_groups]
    inv = 1.0 / (es[0] + es[1] + es[2])
    o_a = None
    for e, o in zip(es, o_groups):
        term = _head_expand(e * inv, e_ref) * o
        o_a = term if o_a is None else o_a + term
    h = _rms_rows(x, ng_ref[...]).astype(BF16)
    gates = jnp.dot(h, wg_ref[...], preferred_element_type=F32)
    z_a, z_b = gates[:, :A_WIDTH], gates[:, A_WIDTH:2 * A_WIDTH]
    g_a, g_b = gates[:, 2 * A_WIDTH:2 * A_WIDTH + D_MODEL], gates[:, 2 * A_WIDTH + D_MODEL:]
    a = (o_a * (z_a * _sigmoid(z_a))).astype(BF16)
    b = (o_b * (z_b * _sigmoid(z_b))).astype(BF16)
    mixed = (_sigmoid(g_a) * jnp.dot(a, pa_ref[...], preferred_element_type=F32)
             + _sigmoid(g_b) * jnp.dot(b, pb_ref[...], preferred_element_type=F32))
    return x + jnp.dot(mixed.astype(BF16), wo_ref[...], preferred_element_type=F32)


def _prompt_merge_body(x_ref, ng_ref, wg_ref, pa_ref, pb_ref, wo_ref, e_ref,
                       o1_ref, l1_ref, o2_ref, l2_ref, o3_ref, l3_ref, ob_ref, y_ref,
                       o_scr, l_scr, *, tt):
    for gi, (o_ref, l_ref, d) in enumerate(((o2_ref, l2_ref, 4), (o3_ref, l3_ref, 16))):
        rows = tt // d
        for r in range(d):
            l_scr[gi, pl.ds(r, rows, stride=d), :] = l_ref[0, r]
            for j in range(A_WIDTH // LANES):
                o_scr[gi, j, pl.ds(r, rows, stride=d), :] = o_ref[0, r, :, j * LANES:(j + 1) * LANES]
    o_groups = [o1_ref[0, 0]] + [
        jnp.concatenate([o_scr[gi, j] for j in range(A_WIDTH // LANES)], axis=1) for gi in range(2)]
    lse_groups = [l1_ref[0, 0], l_scr[0], l_scr[1]]
    y_ref[0] = _merge_math(x_ref[0], ng_ref, wg_ref, pa_ref, pb_ref, wo_ref, e_ref,
                           o_groups, lse_groups, ob_ref[0])


def _prompt_merge(x, ng, w_gates, p_a, p_b, w_o, expand, o_list, lse_list, o_b):
    nb, seq, _ = x.shape
    tt = TOKEN_TILE
    in_specs = [pl.BlockSpec((1, tt, D_MODEL), lambda n, t: (n, t, 0))]
    in_specs += [_resident(a.shape) for a in (ng, w_gates, p_a, p_b, w_o, expand)]
    args = [x, ng, w_gates, p_a, p_b, w_o, expand]
    for g in range(3):
        d = A_DILATIONS[g]
        in_specs.append(pl.BlockSpec((1, d, tt // d, A_WIDTH), lambda n, t: (n, 0, t, 0)))
        in_specs.append(pl.BlockSpec((1, d, tt // d, LANES), lambda n, t: (n, 0, t, 0)))
        args += [o_list[g].reshape(nb, d, seq // d, A_WIDTH), lse_list[g].reshape(nb, d, seq // d, LANES)]
    in_specs.append(pl.BlockSpec((1, tt, A_WIDTH), lambda n, t: (n, t, 0)))
    args.append(o_b)
    return pl.pallas_call(
        functools.partial(_prompt_merge_body, tt=tt),
        grid=(nb, seq // tt),
        in_specs=in_specs,
        out_specs=pl.BlockSpec((1, tt, D_MODEL), lambda n, t: (n, t, 0)),
        out_shape=jax.ShapeDtypeStruct((nb, seq, D_MODEL), F32),
        scratch_shapes=[pltpu.VMEM((2, A_WIDTH // LANES, tt, LANES), F32),
                        pltpu.VMEM((2, tt, LANES), F32)],
        compiler_params=_params(dimension_semantics=("arbitrary", "arbitrary")),
        name="prompt_merge",
    )(*args)


def _sample_proj_body(x_ref, ng_ref, w_ref, bd_ref, gain_ref, q_ref, kv1_ref, kv2_ref, kv3_ref, kvb_ref,
                      y_scr):
    kv_refs = (kv1_ref, kv2_ref, kv3_ref, kvb_ref)
    h = _rms_rows(x_ref[...], ng_ref[...]).astype(BF16)
    for ci, chunk in enumerate(_CHUNKS):
        wd = chunk[1]
        y_scr[:, :wd] = _proj_chunk(h, w_ref, bd_ref, gain_ref, chunk)
        y_t = y_scr[:, :wd].T
        mixer, role = divmod(ci, 3)
        if role == 0:
            q_ref[mixer] = y_t
        else:
            kv_refs[mixer][role - 1] = y_t


def _sample_proj(x, ng, w_qkv, bd, gains):
    dec = x.shape[0]
    out_shape = [jax.ShapeDtypeStruct((4, A_WIDTH, dec), F32)]
    out_shape += [jax.ShapeDtypeStruct((2, A_WIDTH, dec), F32)] * 3
    out_shape += [jax.ShapeDtypeStruct((2, B_KV_WIDTH, dec), F32)]
    args = (x, ng, w_qkv, bd, gains)
    return pl.pallas_call(
        _sample_proj_body,
        grid=(1,),
        in_specs=[_whole(a.shape) for a in args],
        out_specs=[_whole(o.shape) for o in out_shape],
        out_shape=out_shape,
        scratch_shapes=[pltpu.VMEM((dec, A_WIDTH), F32)],
        compiler_params=_params(dimension_semantics=("arbitrary",)),
        name="sample_proj",
    )(*args)


def _lane_column(tile, n):
    lane = lax.broadcasted_iota(jnp.int32, tile.shape, 1)
    return jnp.sum(jnp.where(lane == n, tile, 0.0), axis=1, keepdims=True)


def _cache_stream_body(sink_ref, c_ref, q_ref, kvn_ref, bias_ref, out_ref, o_ref, lse_ref,
                       *, nb, kv_heads, head_order, use_sink):
    window = c_ref.shape[-1]
    rep = N_HEADS // kv_heads
    step = pl.program_id(0)

    @pl.when(step == 0)
    def _():
        o_ref[...] = jnp.zeros_like(o_ref)
        lse_ref[...] = jnp.zeros_like(lse_ref)

    lane_w = lax.broadcasted_iota(jnp.int32, (HEAD_DIM, window), 1)
    dec = o_ref.shape[1]
    lane_b = lax.broadcasted_iota(jnp.int32, (HEAD_DIM, dec), 1)
    lane_r = lax.broadcasted_iota(jnp.int32, (1, dec), 1)

    def one(bi, carry):
        n = step * nb + bi
        new_cols = [[_lane_column(kvn_ref[c, g * HEAD_DIM:(g + 1) * HEAD_DIM, :], n)
                     for g in range(kv_heads)] for c in range(2)]
        for pos in range(N_HEADS):
            head = head_order[pos]
            g = head // rep
            rows = slice(pos * HEAD_DIM, (pos + 1) * HEAD_DIM)
            qcol = _lane_column(q_ref[0, rows, :], n)
            s = jnp.sum(c_ref[bi, 0, g] * qcol, axis=0, keepdims=True) + bias_ref[head:head + 1, :]
            s_new = jnp.sum(new_cols[0][g] * qcol, axis=0, keepdims=True)
            m = jnp.maximum(jnp.max(s, axis=1, keepdims=True), s_new)
            p = jnp.exp(s - m)
            p_new = jnp.exp(s_new - m)
            l = jnp.sum(p, axis=1, keepdims=True) + p_new
            if use_sink:
                l = l + jnp.exp(sink_ref[head] - m)
            o = (jnp.sum(c_ref[bi, 1, g] * p, axis=1, keepdims=True) + new_cols[1][g] * p_new) * (1.0 / l)
            o_ref[rows, :] = jnp.where(lane_b == n, o, o_ref[rows, :])
            lse_ref[pos:pos + 1, :] = jnp.where(lane_r == n, m + jnp.log(l), lse_ref[pos:pos + 1, :])
        for c in range(2):
            for g in range(kv_heads):
                shifted = pltpu.roll(c_ref[bi, c, g], window - 1, axis=1)
                out_ref[bi, c, g] = jnp.where(lane_w == window - 1, new_cols[c][g], shifted)
        return carry

    lax.fori_loop(0, nb, one, 0)


def _cache_stream(sinks, cache_t, q_t, mixer, kv_new, bias_rows, nb, head_order, use_sink):
    dec, _, kv_heads, _, window = cache_t.shape
    nb = min(nb, dec)
    blk = (nb, 2, kv_heads, HEAD_DIM, window)
    return pl.pallas_call(
        functools.partial(_cache_stream_body, nb=nb, kv_heads=kv_heads, head_order=head_order,
                          use_sink=use_sink),
        grid=(dec // nb,),
        in_specs=[pl.BlockSpec(memory_space=pltpu.SMEM),
                  pl.BlockSpec(blk, lambda i: (i, 0, 0, 0, 0)),
                  pl.BlockSpec((1, A_WIDTH, dec), lambda i: (mixer, 0, 0)),
                  _resident(kv_new.shape), _resident(bias_rows.shape)],
        out_specs=[pl.BlockSpec(blk, lambda i: (i, 0, 0, 0, 0)),
                   pl.BlockSpec((A_WIDTH, dec), lambda i: (0, 0)),
                   pl.BlockSpec((LANES, dec), lambda i: (0, 0))],
        out_shape=[jax.ShapeDtypeStruct(cache_t.shape, F32),
                   jax.ShapeDtypeStruct((A_WIDTH, dec), F32),
                   jax.ShapeDtypeStruct((LANES, dec), F32)],
        compiler_params=_params(dimension_semantics=("arbitrary",)),
        name="cache_stream",
    )(sinks, cache_t, q_t, kv_new, bias_rows)


def _sample_merge_body(x_ref, ng_ref, wg_ref, pa_ref, pb_ref, wo_ref, e_ref,
                       o1_ref, l1_ref, o2_ref, l2_ref, o3_ref, l3_ref, ob_ref, y_ref):
    o_groups = [r[...].T for r in (o1_ref, o2_ref, o3_ref)]
    lse_groups = [r[...].T for r in (l1_ref, l2_ref, l3_ref)]
    y_ref[...] = _merge_math(x_ref[...], ng_ref, wg_ref, pa_ref, pb_ref, wo_ref, e_ref,
                             o_groups, lse_groups, ob_ref[...].T)


def _sample_merge(x, ng, w_gates, p_a, p_b, w_o, expand, o_list, lse_list, o_b):
    args = [x, ng, w_gates, p_a, p_b, w_o, expand]
    for o, l in zip(o_list, lse_list):
        args += [o, l]
    args.append(o_b)
    return pl.pallas_call(
        _sample_merge_body,
        grid=(1,),
        in_specs=[_whole(a.shape) for a in args],
        out_specs=_whole(x.shape),
        out_shape=jax.ShapeDtypeStruct(x.shape, F32),
        compiler_params=_params(dimension_semantics=("arbitrary",)),
        name="sample_merge",
    )(*args)


def _buffer_bias_rows(window, dilation):
    r = np.arange(window)
    steps = (window - r) // dilation
    slopes = np.exp2(-np.arange(1, N_HEADS + 1, dtype=np.float32))
    bias = -slopes[:, None] * steps[None].astype(np.float32)
    return np.where((r % dilation == 0)[None], bias, np.float32(NEG)).astype(np.float32)

def _band_bias_table():
    q = np.arange(BLOCK)[:, None]
    kk = np.arange(2 * BLOCK)[None, :]
    delta = q + BLOCK - kk
    valid = (delta >= 0) & (delta <= BLOCK)
    slopes = np.exp2(-np.arange(1, N_HEADS + 1, dtype=np.float32))
    bias = -slopes[:, None, None] * delta[None].astype(np.float32)
    normal = np.where(valid[None], bias, np.float32(NEG))
    first = np.where((valid & (kk >= BLOCK))[None], bias, np.float32(NEG))
    return np.stack([first, normal]).astype(np.float32)


def _block_diag_ones():
    i = np.arange(A_WIDTH)
    return (i[:, None] // HEAD_DIM == i[None, :] // HEAD_DIM).astype(np.float32)


def _head_expand_matrix():
    e = np.zeros((LANES, A_WIDTH), np.float32)
    for hd in range(N_HEADS):
        e[hd, hd * HEAD_DIM:(hd + 1) * HEAD_DIM] = 1.0
    return e


def _b_column_order():
    return np.concatenate([np.arange(HEAD_DIM) + HEAD_DIM * hd for hd in B_HEAD_ORDER])


def _prepare_weights(norm_gain, w_in, qk_norm_a, qk_norm_b, w_branch_a, w_branch_b, w_out):
    w = w_in[0]
    order = _b_column_order()
    b_q0 = 9 * A_WIDTH
    w_qkv = jnp.concatenate([w[:, :b_q0], w[:, b_q0:b_q0 + A_WIDTH][:, order],
                             w[:, b_q0 + A_WIDTH:QKV_WIDTH]], axis=1).astype(BF16)
    g0 = QKV_WIDTH
    w_gates = jnp.concatenate([w[:, g0:g0 + A_WIDTH], w[:, g0 + A_WIDTH:g0 + 2 * A_WIDTH][:, order],
                               w[:, g0 + 2 * A_WIDTH:]], axis=1).astype(BF16)
    scale = HEAD_DIM ** -0.5
    rows = []
    for g in range(3):
        rows.append(jnp.tile(qk_norm_a[0, g, 0], N_HEADS) * scale)
        rows.append(jnp.tile(qk_norm_a[0, g, 1], N_HEADS))
    rows.append(jnp.tile(qk_norm_b[0, 0], N_HEADS) * scale)
    rows.append(jnp.tile(qk_norm_b[0, 1], N_HEADS))
    gains = jnp.stack(rows).astype(F32)
    return dict(ng=norm_gain.astype(F32), w_qkv=w_qkv, w_gates=w_gates, gains=gains,
                p_a=w_branch_a[0].astype(BF16), p_b=w_branch_b[0][order].astype(BF16),
                w_o=w_out[0].astype(BF16))


def _cache_out(t):
    nb, two, width, rows = t.shape
    t = t.reshape(nb, two, width // HEAD_DIM, HEAD_DIM, rows)
    return jnp.transpose(t, (0, 4, 1, 2, 3))[None]


def _prompt_layer(x, wts, sinks, consts):
    nb, seq, _ = x.shape
    outs = _prompt_proj(x, wts["ng"], wts["w_qkv"], consts["bd"], wts["gains"])
    o_list, lse_list, caches = [], [], []
    for g in range(3):
        d = A_DILATIONS[g]
        q, k, v, cache = outs[4 * g:4 * g + 4]
        shape = (nb * d, seq // d, A_WIDTH)
        o, lse = _attn_a(q.reshape(shape), k.reshape(shape), v.reshape(shape), consts["bias"])
        o_list.append(o)
        lse_list.append(lse)
        caches.append(_cache_out(cache))
    o_b = _attn_b(sinks, outs[12], outs[13], outs[14], consts["bias"])
    caches.append(_cache_out(outs[15]))
    y = _prompt_merge(x, wts["ng"], wts["w_gates"], wts["p_a"], wts["p_b"], wts["w_o"], consts["expand"],
                      o_list, lse_list, o_b)
    return y, caches


STREAM_BATCH = (8, 4, 1, 8)


def _sample_layer(x, caches, wts, sinks, consts):
    dec = x.shape[0]
    x2 = x.reshape(dec, D_MODEL)
    q_t, *kv_new = _sample_proj(x2, wts["ng"], wts["w_qkv"], consts["bd"], wts["gains"])
    natural = tuple(range(N_HEADS))
    o_list, lse_list, new_caches = [], [], []
    for mixer in range(4):
        cache_t = jnp.transpose(caches[mixer][0], (0, 2, 3, 4, 1))
        is_b = mixer == 3
        new_t, o_t, lse_t = _cache_stream(
            sinks, cache_t, q_t, mixer, kv_new[mixer], consts["rows"][mixer], STREAM_BATCH[mixer],
            B_HEAD_ORDER if is_b else natural, is_b)
        new_caches.append(jnp.transpose(new_t, (0, 4, 1, 2, 3))[None])
        o_list.append(o_t)
        lse_list.append(lse_t)
    y = _sample_merge(x2, wts["ng"], wts["w_gates"], wts["p_a"], wts["p_b"], wts["w_o"], consts["expand"],
                      o_list[:3], lse_list[:3], o_list[3])
    return y.reshape(dec, 1, D_MODEL), new_caches


def _constants():
    rows = [jnp.asarray(_buffer_bias_rows(w, d)) for w, d in zip(A_WINDOWS + (B_WINDOW,), A_DILATIONS + (1,))]
    return dict(bias=jnp.asarray(_band_bias_table()), bd=jnp.asarray(_block_diag_ones(), BF16),
                expand=jnp.asarray(_head_expand_matrix(), BF16), rows=rows)


def kernel(x_prompt, x_sample, cache_a1_kv, cache_a2_kv, cache_a3_kv, cache_b_kv, norm_gain, w_in,
           qk_norm_a, qk_norm_b, b_sinks, w_branch_a, w_branch_b, w_out):
    wts = _prepare_weights(norm_gain, w_in, qk_norm_a, qk_norm_b, w_branch_a, w_branch_b, w_out)
    consts = _constants()
    sinks = b_sinks[0].astype(F32)
    y_prompt, pc = _prompt_layer(x_prompt, wts, sinks, consts)
    y_sample, sc = _sample_layer(x_sample, (cache_a1_kv, cache_a2_kv, cache_a3_kv, cache_b_kv),
                                 wts, sinks, consts)
    return (y_prompt, y_sample, pc[0], pc[1], pc[2], pc[3], sc[0], sc[1], sc[2], sc[3])
```

```python
import functools

import numpy as np
import jax
import jax.numpy as jnp
from jax import lax
from jax.experimental import pallas as pl
from jax.experimental.pallas import tpu as pltpu

F32 = jnp.float32
BF16 = jnp.bfloat16

D_MODEL = 1024
HEAD_DIM = 64
N_HEADS = 8
A_WINDOWS = (128, 512, 2048)
A_DILATIONS = (1, 4, 16)
B_WINDOW = 128
B_KV_HEADS = 2
A_WIDTH = N_HEADS * HEAD_DIM
B_KV_WIDTH = B_KV_HEADS * HEAD_DIM
QKV_WIDTH = 9 * A_WIDTH + A_WIDTH + 2 * B_KV_WIDTH
BLOCK = 128
Q_ROWS = 512
NORM_EPS = 1e-6
NEG = -1e30
LANES = 128
MXU_WIDTH = 256
TOKEN_TILE = 512
ROW_TILE = 256
VMEM_LIMIT = 56 * 1024 * 1024
B_HEAD_ORDER = (0, 4, 1, 5, 2, 6, 3, 7)
NATURAL_ORDER = tuple(range(N_HEADS))

_CHUNKS = []
for _g in range(3):
    _CHUNKS += [(1536 * _g, 512, 2 * _g), (1536 * _g + 512, 512, 2 * _g + 1), (1536 * _g + 1024, 512, None)]
_CHUNKS += [(4608, 512, 6), (5120, 128, 7), (5248, 128, None)]


def _params(**kw):
    return pltpu.CompilerParams(vmem_limit_bytes=VMEM_LIMIT, **kw)


def _resident(shape):
    nd = len(shape)
    return pl.BlockSpec(shape, lambda *_: (0,) * nd, pipeline_mode=pl.Buffered(1))


def _whole(shape):
    nd = len(shape)
    return pl.BlockSpec(shape, lambda *_: (0,) * nd)


def _rms_rows(x, gain_row):
    ms = jnp.mean(x * x, axis=-1, keepdims=True)
    return (x * lax.rsqrt(ms + NORM_EPS)) * gain_row


def _proj_chunk(h, w_ref, bd_ref, gain_ref, chunk):
    off, wd, gain_row = chunk
    parts = []
    for r0 in range(0, h.shape[0], ROW_TILE):
        acc = jnp.dot(h[r0:r0 + ROW_TILE], w_ref[:, off:off + wd], preferred_element_type=F32)
        if gain_row is not None:
            sq = (acc * acc).astype(BF16)
            step = min(wd, MXU_WIDTH)
            ss = jnp.concatenate(
                [jnp.dot(sq[:, c0:c0 + step], bd_ref[:step, :step], preferred_element_type=F32)
                 for c0 in range(0, wd, step)], axis=1)
            acc = (acc * lax.rsqrt(ss * (1.0 / HEAD_DIM) + NORM_EPS)) * gain_ref[gain_row:gain_row + 1, :wd]
        parts.append(acc)
    return parts[0] if len(parts) == 1 else jnp.concatenate(parts, axis=0)


def _sigmoid(z):
    return 1.0 / (1.0 + jnp.exp(-z))


def _prompt_proj_body(x_ref, ng_ref, w_ref, bd_ref, gain_ref, *refs, tt):
    outs = refs[:12]
    perm_ref = refs[12]
    h = _rms_rows(x_ref[0], ng_ref[...]).astype(BF16)
    for ci, chunk in enumerate(_CHUNKS):
        y = _proj_chunk(h, w_ref, bd_ref, gain_ref, chunk)
        mixer, role = divmod(ci, 3)
        o_ref = outs[ci]
        d = A_DILATIONS[mixer] if mixer < 3 else 1
        if mixer == 3:
            o_ref[0] = y.astype(BF16)
        elif d == 1:
            o_ref[0, 0] = y.astype(BF16)
        else:
            rows = tt // d
            for j in range(A_WIDTH // LANES):
                perm_ref[j] = y[:, j * LANES:(j + 1) * LANES]
            for r in range(d):
                for j in range(A_WIDTH // LANES):
                    o_ref[0, r, :, j * LANES:(j + 1) * LANES] = (
                        perm_ref[j, pl.ds(r, rows, stride=d), :].astype(BF16))


def _prompt_proj(x, ng, w_qkv, bd, gains):
    nb, seq, _ = x.shape
    tt = TOKEN_TILE
    out_shape, out_specs = [], []
    for g in range(3):
        d = A_DILATIONS[g]
        for _ in range(3):
            out_shape.append(jax.ShapeDtypeStruct((nb, d, seq // d, A_WIDTH), BF16))
            out_specs.append(pl.BlockSpec((1, d, tt // d, A_WIDTH), lambda n, t: (n, 0, t, 0)))
    for wd in (A_WIDTH, B_KV_WIDTH, B_KV_WIDTH):
        out_shape.append(jax.ShapeDtypeStruct((nb, seq, wd), BF16))
        out_specs.append(pl.BlockSpec((1, tt, wd), lambda n, t: (n, t, 0)))
    return pl.pallas_call(
        functools.partial(_prompt_proj_body, tt=tt),
        grid=(nb, seq // tt),
        in_specs=[pl.BlockSpec((1, tt, D_MODEL), lambda n, t: (n, t, 0)),
                  _resident(ng.shape), _resident(w_qkv.shape), _resident(bd.shape), _resident(gains.shape)],
        out_specs=out_specs,
        out_shape=out_shape,
        scratch_shapes=[pltpu.VMEM((A_WIDTH // LANES, tt, LANES), F32)],
        compiler_params=_params(dimension_semantics=("arbitrary", "arbitrary")),
        name="prompt_proj",
    )(x, ng, w_qkv, bd, gains)


_KV_ROWS = tuple((1024 * g + 512 * c, 512, c == 0) for g in range(3) for c in range(2)) + (
    (3072, 128, True), (3200, 128, False))


def _kv_transposed(h, wt_ref, gain_ref, rows):
    r0, nr, normed = rows
    y = lax.dot_general(wt_ref[r0:r0 + nr, :], h, (((1,), (1,)), ((), ())), preferred_element_type=F32)
    if normed:
        y3 = y.reshape(nr // HEAD_DIM, HEAD_DIM, y.shape[1])
        ms = jnp.mean(y3 * y3, axis=1, keepdims=True)
        y = (y3 * lax.rsqrt(ms + NORM_EPS)).reshape(y.shape) * gain_ref[r0:r0 + nr, :y.shape[1]]
    return y


def _prompt_cache_body(x_ref, ng_ref, wt_ref, gain_ref, c1_ref, c2_ref, c3_ref, cb_ref, *, tt):
    j = pl.program_id(1)
    last = pl.num_programs(1) - 1
    h = _rms_rows(x_ref[0], ng_ref[...]).astype(BF16)
    for c in range(2):
        c3_ref[0, c] = _kv_transposed(h, wt_ref, gain_ref, _KV_ROWS[4 + c])

    @pl.when(j == last)
    def _():
        tail = h[tt - BLOCK:]
        for c in range(2):
            c2_ref[0, c] = _kv_transposed(h, wt_ref, gain_ref, _KV_ROWS[2 + c])
            c1_ref[0, c] = _kv_transposed(tail, wt_ref, gain_ref, _KV_ROWS[c])
            cb_ref[0, c] = _kv_transposed(tail, wt_ref, gain_ref, _KV_ROWS[6 + c])


def _prompt_cache(x, ng, w_kv_t, gain_cols):
    nb, seq, _ = x.shape
    tt = TOKEN_TILE
    assert A_WINDOWS == (BLOCK, tt, 4 * tt) and B_WINDOW == BLOCK
    n_tiles = A_WINDOWS[2] // tt
    first = seq // tt - n_tiles
    out_shape = [jax.ShapeDtypeStruct((nb, 2, A_WIDTH, w), F32) for w in A_WINDOWS]
    out_shape.append(jax.ShapeDtypeStruct((nb, 2, B_KV_WIDTH, B_WINDOW), F32))
    out_specs = [pl.BlockSpec((1, 2, A_WIDTH, BLOCK), lambda n, j: (n, 0, 0, 0)),
                 pl.BlockSpec((1, 2, A_WIDTH, tt), lambda n, j: (n, 0, 0, 0)),
                 pl.BlockSpec((1, 2, A_WIDTH, tt), lambda n, j: (n, 0, 0, j)),
                 pl.BlockSpec((1, 2, B_KV_WIDTH, BLOCK), lambda n, j: (n, 0, 0, 0))]
    return pl.pallas_call(
        functools.partial(_prompt_cache_body, tt=tt),
        grid=(nb, n_tiles),
        in_specs=[pl.BlockSpec((1, tt, D_MODEL), lambda n, j: (n, first + j, 0)),
                  _resident(ng.shape), _resident(w_kv_t.shape), _resident(gain_cols.shape)],
        out_specs=out_specs,
        out_shape=out_shape,
        compiler_params=_params(dimension_semantics=("arbitrary", "arbitrary")),
        name="prompt_cache",
    )(x, ng, w_kv_t, gain_cols)


def _stat_lane(pos):
    return pos if pos % 2 else HEAD_DIM + pos


def _softmax_block(qm, kk, bias, v_ext):
    s = lax.dot_general(qm, kk, (((1,), (1,)), ((), ())), preferred_element_type=F32) + bias
    m = jnp.max(s, axis=-1, keepdims=True)
    p = jnp.exp(s - m).astype(BF16)
    return m, jnp.dot(p, v_ext, preferred_element_type=F32)


def _attn_body(q_ref, kp_ref, kc_ref, vp_ref, vc_ref, bias_ref, o_ref, ml_ref, *, shared_kv, head_of_pos):
    n_blocks = Q_ROWS // BLOCK
    first = jnp.minimum(pl.program_id(1), 1)
    lane = lax.broadcasted_iota(jnp.int32, (BLOCK, LANES), 1)
    lo = lane < HEAD_DIM
    lo_kv = lax.broadcasted_iota(jnp.int32, (BLOCK + Q_ROWS, LANES), 1) < HEAD_DIM
    one = jnp.ones((), BF16)
    m_tiles = [jnp.zeros((BLOCK, LANES), F32) for _ in range(n_blocks)]
    l_tiles = [jnp.ones((BLOCK, LANES), F32) for _ in range(n_blocks)]

    def load_kv(cols):
        k_all = jnp.concatenate([kp_ref[0, :, cols], kc_ref[0, :, cols]], axis=0)
        v_all = jnp.concatenate([vp_ref[0, :, cols], vc_ref[0, :, cols]], axis=0)
        return k_all, (jnp.where(lo_kv, v_all, one), jnp.where(lo_kv, one, v_all))

    if shared_kv:
        k_all, v_ext = load_kv(slice(0, LANES))
    for i in range(N_HEADS // 2):
        cols = slice(i * LANES, (i + 1) * LANES)
        if not shared_kv:
            k_all, v_ext = load_kv(cols)
        for b in range(n_blocks):
            sel = first if b == 0 else 1
            keys = slice(b * BLOCK, (b + 2) * BLOCK)
            rows = slice(b * BLOCK, (b + 1) * BLOCK)
            qp = q_ref[0, rows, cols]
            m_pair, pv_pair = [], []
            for hh in range(2):
                qm = jnp.where(lo if hh == 0 else ~lo, qp, jnp.zeros_like(qp))
                m, pv = _softmax_block(qm, k_all[keys], bias_ref[sel, head_of_pos[2 * i + hh]], v_ext[hh][keys])
                m_pair.append(m)
                pv_pair.append(pv)
            o_ref[0, rows, cols] = jnp.where(lo, pv_pair[0], pv_pair[1])
            for hh in range(2):
                at = lane == _stat_lane(2 * i + hh)
                m_tiles[b] = jnp.where(at, m_pair[hh], m_tiles[b])
                l_tiles[b] = jnp.where(at, pv_pair[hh], l_tiles[b])
    for b in range(n_blocks):
        ml_ref[0, b * BLOCK:(b + 1) * BLOCK, :LANES] = m_tiles[b]
        ml_ref[0, b * BLOCK:(b + 1) * BLOCK, LANES:] = l_tiles[b]


def _attention(q, k, v, bias, head_of_pos):
    ns, steps, _ = q.shape
    kv_width = k.shape[-1]
    ratio = Q_ROWS // BLOCK
    qspec = pl.BlockSpec((1, Q_ROWS, A_WIDTH), lambda s, u: (s, u, 0))
    cur = pl.BlockSpec((1, Q_ROWS, kv_width), lambda s, u: (s, u, 0))
    prev = pl.BlockSpec((1, BLOCK, kv_width), lambda s, u: (s, jnp.maximum(u * ratio - 1, 0), 0))
    return pl.pallas_call(
        functools.partial(_attn_body, shared_kv=kv_width == LANES, head_of_pos=head_of_pos),
        grid=(ns, steps // Q_ROWS),
        in_specs=[qspec, prev, cur, prev, cur, _resident(bias.shape)],
        out_specs=[qspec, pl.BlockSpec((1, Q_ROWS, 2 * LANES), lambda s, u: (s, u, 0))],
        out_shape=[jax.ShapeDtypeStruct((ns, steps, A_WIDTH), F32),
                   jax.ShapeDtypeStruct((ns, steps, 2 * LANES), F32)],
        compiler_params=_params(dimension_semantics=("arbitrary", "arbitrary")),
        name="attention",
    )(q, k, k, v, v, bias)


def _head_expand(w, e_ref):
    return jnp.dot(w.astype(BF16), e_ref[...], preferred_element_type=F32)


def _merge_math(x, ng_ref, wg_ref, pa_ref, pb_ref, wo_ref, e_ref, sink_ref, o_groups, ml_groups, o_b, ml_b):
    ms = [ml[:, :LANES] for ml in ml_groups]
    ls = [ml[:, LANES:] for ml in ml_groups]
    big = jnp.maximum(jnp.maximum(ms[0], ms[1]), ms[2])
    es = [jnp.exp(m - big) for m in ms]
    inv = 1.0 / (es[0] * ls[0] + es[1] * ls[1] + es[2] * ls[2])
    o_a = None
    for e, o in zip(es, o_groups):
        term = _head_expand(e * inv, e_ref) * o
        o_a = term if o_a is None else o_a + term
    inv_b = 1.0 / (ml_b[:, LANES:] + jnp.exp(sink_ref[...] - ml_b[:, :LANES]))
    o_b = _head_expand(inv_b, e_ref) * o_b
    h = _rms_rows(x, ng_ref[...]).astype(BF16)
    gates = jnp.dot(h, wg_ref[...], preferred_element_type=F32)
    z_a, z_b = gates[:, :A_WIDTH], gates[:, A_WIDTH:2 * A_WIDTH]
    g_a, g_b = gates[:, 2 * A_WIDTH:2 * A_WIDTH + D_MODEL], gates[:, 2 * A_WIDTH + D_MODEL:]
    a = (o_a * (z_a * _sigmoid(z_a))).astype(BF16)
    b = (o_b * (z_b * _sigmoid(z_b))).astype(BF16)
    mixed = (_sigmoid(g_a) * jnp.dot(a, pa_ref[...], preferred_element_type=F32)
             + _sigmoid(g_b) * jnp.dot(b, pb_ref[...], preferred_element_type=F32))
    return x + jnp.dot(mixed.astype(BF16), wo_ref[...], preferred_element_type=F32)


def _prompt_merge_body(x_ref, ng_ref, wg_ref, pa_ref, pb_ref, wo_ref, e_ref, sink_ref,
                       o1_ref, l1_ref, o2_ref, l2_ref, o3_ref, l3_ref, ob_ref, lb_ref, y_ref,
                       o_scr, l_scr, *, tt):
    for gi, (o_ref, l_ref, d) in enumerate(((o2_ref, l2_ref, 4), (o3_ref, l3_ref, 16))):
        rows = tt // d
        for r in range(d):
            for j in range(2):
                l_scr[gi, j, pl.ds(r, rows, stride=d), :] = l_ref[0, r, :, j * LANES:(j + 1) * LANES]
            for j in range(A_WIDTH // LANES):
                o_scr[gi, j, pl.ds(r, rows, stride=d), :] = o_ref[0, r, :, j * LANES:(j + 1) * LANES]
    o_groups = [o1_ref[0, 0]] + [
        jnp.concatenate([o_scr[gi, j] for j in range(A_WIDTH // LANES)], axis=1) for gi in range(2)]
    ml_groups = [l1_ref[0, 0]] + [jnp.concatenate([l_scr[gi, 0], l_scr[gi, 1]], axis=1) for gi in range(2)]
    y_ref[0] = _merge_math(x_ref[0], ng_ref, wg_ref, pa_ref, pb_ref, wo_ref, e_ref, sink_ref,
                           o_groups, ml_groups, ob_ref[0], lb_ref[0])


def _prompt_merge(x, ng, w_gates, p_a, p_b, w_o, expand, sink_row, o_list, ml_list, o_b, ml_b):
    nb, seq, _ = x.shape
    tt = TOKEN_TILE
    in_specs = [pl.BlockSpec((1, tt, D_MODEL), lambda n, t: (n, t, 0))]
    in_specs += [_resident(a.shape) for a in (ng, w_gates, p_a, p_b, w_o, expand, sink_row)]
    args = [x, ng, w_gates, p_a, p_b, w_o, expand, sink_row]
    for g in range(3):
        d = A_DILATIONS[g]
        in_specs.append(pl.BlockSpec((1, d, tt // d, A_WIDTH), lambda n, t: (n, 0, t, 0)))
        in_specs.append(pl.BlockSpec((1, d, tt // d, 2 * LANES), lambda n, t: (n, 0, t, 0)))
        args += [o_list[g].reshape(nb, d, seq // d, A_WIDTH), ml_list[g].reshape(nb, d, seq // d, 2 * LANES)]
    in_specs.append(pl.BlockSpec((1, tt, A_WIDTH), lambda n, t: (n, t, 0)))
    in_specs.append(pl.BlockSpec((1, tt, 2 * LANES), lambda n, t: (n, t, 0)))
    args += [o_b, ml_b]
    return pl.pallas_call(
        functools.partial(_prompt_merge_body, tt=tt),
        grid=(nb, seq // tt),
        in_specs=in_specs,
        out_specs=pl.BlockSpec((1, tt, D_MODEL), lambda n, t: (n, t, 0)),
        out_shape=jax.ShapeDtypeStruct((nb, seq, D_MODEL), F32),
        scratch_shapes=[pltpu.VMEM((2, A_WIDTH // LANES, tt, LANES), F32),
                        pltpu.VMEM((2, 2, tt, LANES), F32)],
        compiler_params=_params(dimension_semantics=("arbitrary", "arbitrary")),
        name="prompt_merge",
    )(*args)


def _sample_proj_body(x_ref, ng_ref, w_ref, bd_ref, gain_ref, q_ref, kva_ref, kvb_ref, kvta_ref, kvtb_ref,
                      y_scr):
    dec = x_ref.shape[0]
    h = _rms_rows(x_ref[...], ng_ref[...]).astype(BF16)
    kvta_ref[...] = jnp.zeros_like(kvta_ref)
    kvtb_ref[...] = jnp.zeros_like(kvtb_ref)
    for ci, chunk in enumerate(_CHUNKS):
        wd = chunk[1]
        y = _proj_chunk(h, w_ref, bd_ref, gain_ref, chunk)
        mixer, role = divmod(ci, 3)
        if role == 0:
            q_ref[:, mixer * A_WIDTH:(mixer + 1) * A_WIDTH] = y
            continue
        y_scr[:, :wd] = y
        y_t = y_scr[:, :wd].T
        if mixer < 3:
            kva_ref[mixer, role - 1] = y
            kvta_ref[mixer, role - 1, :, :dec] = y_t
        else:
            kvb_ref[role - 1] = y
            kvtb_ref[role - 1, :, :dec] = y_t


def _sample_proj(x, ng, w_qkv, bd, gains):
    dec = x.shape[0]
    assert dec <= LANES
    out_shape = [jax.ShapeDtypeStruct((dec, 4 * A_WIDTH), F32),
                 jax.ShapeDtypeStruct((3, 2, dec, A_WIDTH), F32),
                 jax.ShapeDtypeStruct((2, dec, B_KV_WIDTH), F32),
                 jax.ShapeDtypeStruct((3, 2, A_WIDTH, LANES), F32),
                 jax.ShapeDtypeStruct((2, B_KV_WIDTH, LANES), F32)]
    args = (x, ng, w_qkv, bd, gains)
    return pl.pallas_call(
        _sample_proj_body,
        grid=(1,),
        in_specs=[_whole(a.shape) for a in args],
        out_specs=[_whole(o.shape) for o in out_shape],
        out_shape=out_shape,
        scratch_shapes=[pltpu.VMEM((dec, A_WIDTH), F32)],
        compiler_params=_params(dimension_semantics=("arbitrary",)),
        name="sample_proj",
    )(*args)


def _cache_stream_body(c_ref, q_ref, kn_ref, vn_ref, kvt_ref, bias_ref, out_ref, o_ref, ml_ref,
                       *, nb, kv_heads):
    window = c_ref.shape[-1]
    reps = N_HEADS // kv_heads
    step = pl.program_id(0)
    head_lane = lax.broadcasted_iota(jnp.int32, (N_HEADS, A_WIDTH), 1) // HEAD_DIM
    head_row = lax.broadcasted_iota(jnp.int32, (N_HEADS, A_WIDTH), 0)
    own = head_lane == head_row
    srow = lax.broadcasted_iota(jnp.int32, (N_HEADS, LANES), 0)
    slane = lax.broadcasted_iota(jnp.int32, (N_HEADS, LANES), 1)
    stat = slane == jnp.where(srow % 2 == 1, srow, srow + HEAD_DIM)
    used = jnp.sum(stat.astype(F32), axis=0, keepdims=True) > 0.0
    lane_w = lax.broadcasted_iota(jnp.int32, (HEAD_DIM, window), 1)

    def stacked(tiles):
        return jnp.concatenate([tiles[pos % kv_heads] for pos in range(N_HEADS)], axis=0)

    def tiled(row):
        return row if reps == 1 else jnp.concatenate([row] * reps, axis=1)

    for bi in range(nb):
        n = step * nb + bi
        q_row = q_ref[n]
        kn_row = tiled(kn_ref[n])
        vn_row = tiled(vn_ref[n])
        q_bd = jnp.where(own, q_row, 0.0)
        k_all = stacked([c_ref[bi, 0, g].astype(BF16) for g in range(kv_heads)])
        v_all = stacked([c_ref[bi, 1, g].astype(BF16) for g in range(kv_heads)])
        s = jnp.dot(q_bd.astype(BF16), k_all, preferred_element_type=F32) + bias_ref[...]
        s_new = jnp.sum(q_bd * kn_row, axis=1, keepdims=True)
        m = jnp.maximum(jnp.max(s, axis=1, keepdims=True), s_new)
        p = jnp.exp(s - m)
        p_new = jnp.exp(s_new - m)
        l = jnp.sum(p, axis=1, keepdims=True) + p_new
        o_t = lax.dot_general(p.astype(BF16), v_all, (((1,), (1,)), ((), ())), preferred_element_type=F32)
        o_ref[n] = jnp.sum(jnp.where(own, o_t + p_new * vn_row, 0.0), axis=0, keepdims=True)
        ml_ref[n, :, :LANES] = jnp.sum(jnp.where(stat, m, 0.0), axis=0, keepdims=True)
        ml_ref[n, :, LANES:] = jnp.where(used, jnp.sum(jnp.where(stat, l, 0.0), axis=0, keepdims=True), 1.0)
        for c in range(2):
            for g in range(kv_heads):
                new = pltpu.roll(kvt_ref[c, g * HEAD_DIM:(g + 1) * HEAD_DIM, :], LANES - 1 - n, axis=1)
                if window > LANES:
                    new = jnp.concatenate([new] * (window // LANES), axis=1)
                shifted = pltpu.roll(c_ref[bi, c, g], window - 1, axis=1)
                out_ref[bi, c, g] = jnp.where(lane_w == window - 1, new, shifted)


def _cache_stream(cache_t, q, mixer, kv_new, kv_new_t, bias_rows, nb):
    dec, _, kv_heads, _, window = cache_t.shape
    nb = min(nb, dec)
    blk = (nb, 2, kv_heads, HEAD_DIM, window)
    kv_width = kv_heads * HEAD_DIM
    new_o, o, ml = pl.pallas_call(
        functools.partial(_cache_stream_body, nb=nb, kv_heads=kv_heads),
        grid=(dec // nb,),
        in_specs=[pl.BlockSpec(blk, lambda i: (i, 0, 0, 0, 0)),
                  pl.BlockSpec((dec, 1, A_WIDTH), lambda i: (0, 0, mixer)),
                  pl.BlockSpec((None, dec, 1, kv_width), lambda i: (0, 0, 0, 0)),
                  pl.BlockSpec((None, dec, 1, kv_width), lambda i: (1, 0, 0, 0)),
                  _resident(kv_new_t.shape), _resident(bias_rows.shape)],
        out_specs=[pl.BlockSpec(blk, lambda i: (i, 0, 0, 0, 0)),
                   pl.BlockSpec((dec, 1, A_WIDTH), lambda i: (0, 0, 0)),
                   pl.BlockSpec((dec, 1, 2 * LANES), lambda i: (0, 0, 0))],
        out_shape=[jax.ShapeDtypeStruct(cache_t.shape, F32),
                   jax.ShapeDtypeStruct((dec, 1, A_WIDTH), F32),
                   jax.ShapeDtypeStruct((dec, 1, 2 * LANES), F32)],
        compiler_params=_params(dimension_semantics=("arbitrary",)),
        name="cache_stream",
    )(cache_t, q, kv_new, kv_new, kv_new_t, bias_rows)
    return new_o, o.reshape(dec, A_WIDTH), ml.reshape(dec, 2 * LANES)


def _sample_merge_body(x_ref, ng_ref, wg_ref, pa_ref, pb_ref, wo_ref, e_ref, sink_ref,
                       o1_ref, l1_ref, o2_ref, l2_ref, o3_ref, l3_ref, ob_ref, lb_ref, y_ref):
    y_ref[...] = _merge_math(x_ref[...], ng_ref, wg_ref, pa_ref, pb_ref, wo_ref, e_ref, sink_ref,
                             [o1_ref[...], o2_ref[...], o3_ref[...]],
                             [l1_ref[...], l2_ref[...], l3_ref[...]], ob_ref[...], lb_ref[...])


def _sample_merge(x, ng, w_gates, p_a, p_b, w_o, expand, sink_row, o_list, ml_list):
    args = [x, ng, w_gates, p_a, p_b, w_o, expand, sink_row]
    for o, l in zip(o_list, ml_list):
        args += [o, l]
    return pl.pallas_call(
        _sample_merge_body,
        grid=(1,),
        in_specs=[_whole(a.shape) for a in args],
        out_specs=_whole(x.shape),
        out_shape=jax.ShapeDtypeStruct(x.shape, F32),
        compiler_params=_params(dimension_semantics=("arbitrary",)),
        name="sample_merge",
    )(*args)


def _buffer_bias_rows(window, dilation):
    r = np.arange(window)
    steps = (window - r) // dilation
    slopes = np.exp2(-np.arange(1, N_HEADS + 1, dtype=np.float32))
    bias = -slopes[:, None] * steps[None].astype(np.float32)
    return np.where((r % dilation == 0)[None], bias, np.float32(NEG)).astype(np.float32)


def _band_bias_table():
    q = np.arange(BLOCK)[:, None]
    kk = np.arange(2 * BLOCK)[None, :]
    delta = q + BLOCK - kk
    valid = (delta >= 0) & (delta <= BLOCK)
    slopes = np.exp2(-np.arange(1, N_HEADS + 1, dtype=np.float32))
    bias = -slopes[:, None, None] * delta[None].astype(np.float32)
    normal = np.where(valid[None], bias, np.float32(NEG))
    first = np.where((valid & (kk >= BLOCK))[None], bias, np.float32(NEG))
    return np.stack([first, normal]).astype(np.float32)


def _block_diag_ones():
    i = np.arange(MXU_WIDTH)
    return (i[:, None] // HEAD_DIM == i[None, :] // HEAD_DIM).astype(np.float32)


def _head_expand_matrix():
    e = np.zeros((LANES, A_WIDTH), np.float32)
    for pos in range(N_HEADS):
        e[_stat_lane(pos), pos * HEAD_DIM:(pos + 1) * HEAD_DIM] = 1.0
    return e


def _b_column_order():
    return np.concatenate([np.arange(HEAD_DIM) + HEAD_DIM * hd for hd in B_HEAD_ORDER])


def _prepare_weights(norm_gain, w_in, qk_norm_a, qk_norm_b, b_sinks, w_branch_a, w_branch_b, w_out):
    w = w_in[0]
    order = _b_column_order()
    b_q0 = 9 * A_WIDTH
    w_qkv = jnp.concatenate([w[:, :b_q0], w[:, b_q0:b_q0 + A_WIDTH][:, order],
                             w[:, b_q0 + A_WIDTH:QKV_WIDTH]], axis=1).astype(BF16)
    g0 = QKV_WIDTH
    w_gates = jnp.concatenate([w[:, g0:g0 + A_WIDTH], w[:, g0 + A_WIDTH:g0 + 2 * A_WIDTH][:, order],
                               w[:, g0 + 2 * A_WIDTH:]], axis=1).astype(BF16)
    scale = HEAD_DIM ** -0.5
    rows = []
    for g in range(3):
        rows.append(jnp.tile(qk_norm_a[0, g, 0], N_HEADS) * scale)
        rows.append(jnp.tile(qk_norm_a[0, g, 1], N_HEADS))
    rows.append(jnp.tile(qk_norm_b[0, 0], N_HEADS) * scale)
    rows.append(jnp.tile(qk_norm_b[0, 1], N_HEADS))
    gains = jnp.stack(rows).astype(F32)
    lanes = np.array([_stat_lane(pos) for pos in range(N_HEADS)])
    sink_row = jnp.zeros((1, LANES), F32).at[0, lanes].set(b_sinks[0][np.array(B_HEAD_ORDER)].astype(F32))
    kv_cols = np.concatenate([np.arange(1536 * g + 512, 1536 * g + 1536) for g in range(3)]
                             + [np.arange(5120, 5376)])
    w_kv_t = w[:, kv_cols].T.astype(BF16)
    ones = jnp.ones((A_WIDTH,), F32)
    gain_rows = [r for g in range(3) for r in (jnp.tile(qk_norm_a[0, g, 1], N_HEADS), ones)]
    gain_rows += [jnp.tile(qk_norm_b[0, 1], B_KV_HEADS), ones[:B_KV_WIDTH]]
    gain_cols = jnp.broadcast_to(jnp.concatenate(gain_rows).astype(F32)[:, None], (kv_cols.size, TOKEN_TILE))
    return dict(ng=norm_gain.astype(F32), w_qkv=w_qkv, w_gates=w_gates, gains=gains, sink_row=sink_row,
                w_kv_t=w_kv_t, gain_cols=gain_cols,
                p_a=w_branch_a[0].astype(BF16), p_b=w_branch_b[0][order].astype(BF16),
                w_o=w_out[0].astype(BF16))


def _cache_out(t):
    nb, two, width, rows = t.shape
    t = t.reshape(nb, two, width // HEAD_DIM, HEAD_DIM, rows)
    return jnp.transpose(t, (0, 4, 1, 2, 3))[None]


def _prompt_layer(x, wts, consts):
    nb, seq, _ = x.shape
    outs = _prompt_proj(x, wts["ng"], wts["w_qkv"], consts["bd"], wts["gains"])
    caches = [_cache_out(c) for c in _prompt_cache(x, wts["ng"], wts["w_kv_t"], wts["gain_cols"])]
    o_list, ml_list = [], []
    for g in range(3):
        d = A_DILATIONS[g]
        q, k, v = outs[3 * g:3 * g + 3]
        shape = (nb * d, seq // d, A_WIDTH)
        o, ml = _attention(q.reshape(shape), k.reshape(shape), v.reshape(shape), consts["bias"], NATURAL_ORDER)
        o_list.append(o)
        ml_list.append(ml)
    o_b, ml_b = _attention(outs[9], outs[10], outs[11], consts["bias"], B_HEAD_ORDER)
    y = _prompt_merge(x, wts["ng"], wts["w_gates"], wts["p_a"], wts["p_b"], wts["w_o"], consts["expand"],
                      wts["sink_row"], o_list, ml_list, o_b, ml_b)
    return y, caches


STREAM_BATCH = (8, 4, 1, 8)


def _sample_layer(x, caches, wts, consts):
    dec = x.shape[0]
    x2 = x.reshape(dec, D_MODEL)
    q, kv_a, kv_b, kvt_a, kvt_b = _sample_proj(x2, wts["ng"], wts["w_qkv"], consts["bd"], wts["gains"])
    o_list, ml_list, new_caches = [], [], []
    for mixer in range(4):
        cache_t = jnp.transpose(caches[mixer][0], (0, 2, 3, 4, 1))
        kv_new, kv_new_t = (kv_b, kvt_b) if mixer == 3 else (kv_a[mixer], kvt_a[mixer])
        new_t, o, ml = _cache_stream(cache_t, q[:, None, :], mixer, kv_new[:, :, None, :], kv_new_t,
                                     consts["rows"][mixer], STREAM_BATCH[mixer])
        new_caches.append(jnp.transpose(new_t, (0, 4, 1, 2, 3))[None])
        o_list.append(o)
        ml_list.append(ml)
    y = _sample_merge(x2, wts["ng"], wts["w_gates"], wts["p_a"], wts["p_b"], wts["w_o"], consts["expand"],
                      wts["sink_row"], o_list, ml_list)
    return y.reshape(dec, 1, D_MODEL), new_caches


def _constants():
    assert all(B_HEAD_ORDER[pos] // (N_HEADS // B_KV_HEADS) == pos % B_KV_HEADS for pos in range(N_HEADS))
    rows = [jnp.asarray(_buffer_bias_rows(w, d)) for w, d in zip(A_WINDOWS, A_DILATIONS)]
    rows.append(jnp.asarray(_buffer_bias_rows(B_WINDOW, 1)[np.array(B_HEAD_ORDER)]))
    return dict(bias=jnp.asarray(_band_bias_table()), bd=jnp.asarray(_block_diag_ones(), BF16),
                expand=jnp.asarray(_head_expand_matrix(), BF16), rows=rows)


def kernel(x_prompt, x_sample, cache_a1_kv, cache_a2_kv, cache_a3_kv, cache_b_kv, norm_gain, w_in,
           qk_norm_a, qk_norm_b, b_sinks, w_branch_a, w_branch_b, w_out):
    wts = _prepare_weights(norm_gain, w_in, qk_norm_a, qk_norm_b, b_sinks, w_branch_a, w_branch_b, w_out)
    consts = _constants()
    y_prompt, pc = _prompt_layer(x_prompt, wts, consts)
    y_sample, sc = _sample_layer(x_sample, (cache_a1_kv, cache_a2_kv, cache_a3_kv, cache_b_kv), wts, consts)
    return (y_prompt, y_sample, pc[0], pc[1], pc[2], pc[3], sc[0], sc[1], sc[2], sc[3])
```

```python
import functools
from typing import Callable, NamedTuple, Sequence

import numpy as np
import jax
import jax.numpy as jnp
from jax import lax
from jax.experimental import pallas as pl
from jax.experimental.pallas import tpu as pltpu

F32 = jnp.float32
BF16 = jnp.bfloat16

D_MODEL = 1024
HEAD_DIM = 64
N_HEADS = 8
A_WINDOWS = (128, 512, 2048)
A_DILATIONS = (1, 4, 16)
B_WINDOW = 128
B_KV_HEADS = 2
A_WIDTH = N_HEADS * HEAD_DIM
B_KV_WIDTH = B_KV_HEADS * HEAD_DIM
QKV_WIDTH = 9 * A_WIDTH + A_WIDTH + 2 * B_KV_WIDTH
BLOCK = 128
Q_ROWS = 512
NORM_EPS = 1e-6
NEG = -1e30
LANES = 128
MXU_WIDTH = 256
TOKEN_TILE = 256
CACHE_TILE = 512
ROW_TILE = 256
VMEM_LIMIT = 56 * 1024 * 1024
B_HEAD_ORDER = (0, 4, 1, 5, 2, 6, 3, 7)
NATURAL_ORDER = tuple(range(N_HEADS))

_CHUNKS = []
for _g in range(3):
    _CHUNKS += [(1536 * _g, 512, 2 * _g), (1536 * _g + 512, 512, 2 * _g + 1), (1536 * _g + 1024, 512, None)]
_CHUNKS += [(4608, 512, 6), (5120, 128, 7), (5248, 128, None)]


def _params(**kw):
    return pltpu.CompilerParams(vmem_limit_bytes=VMEM_LIMIT, **kw)


def _resident(shape):
    nd = len(shape)
    return pl.BlockSpec(shape, lambda *_: (0,) * nd, pipeline_mode=pl.Buffered(1))


def _whole(shape):
    nd = len(shape)
    return pl.BlockSpec(shape, lambda *_: (0,) * nd)


class _Job(NamedTuple):
    body: Callable
    steps: int
    in_specs: Sequence
    out_specs: Sequence
    out_shape: Sequence
    scratch_shapes: Sequence
    args: Sequence
    name: str


def _run(job):
    return pl.pallas_call(
        job.body, grid=(job.steps,), in_specs=list(job.in_specs), out_specs=list(job.out_specs),
        out_shape=list(job.out_shape), scratch_shapes=list(job.scratch_shapes),
        compiler_params=_params(dimension_semantics=("arbitrary",)), name=job.name)(*job.args)


def _run_pair(a, b):
    if a.steps != b.steps:
        return _run(a), _run(b)
    n_in = (len(a.in_specs), len(b.in_specs))
    n_out = (len(a.out_specs), len(b.out_specs))
    n_scr = len(a.scratch_shapes)

    def body(*refs):
        ins, rest = refs[:sum(n_in)], refs[sum(n_in):]
        outs, scr = rest[:sum(n_out)], rest[sum(n_out):]
        a.body(*ins[:n_in[0]], *outs[:n_out[0]], *scr[:n_scr])
        b.body(*ins[n_in[0]:], *outs[n_out[0]:], *scr[n_scr:])

    outs = pl.pallas_call(
        body, grid=(a.steps,), in_specs=list(a.in_specs) + list(b.in_specs),
        out_specs=list(a.out_specs) + list(b.out_specs), out_shape=list(a.out_shape) + list(b.out_shape),
        scratch_shapes=list(a.scratch_shapes) + list(b.scratch_shapes),
        compiler_params=_params(dimension_semantics=("arbitrary",)), name=a.name + "_" + b.name,
    )(*a.args, *b.args)
    return outs[:n_out[0]], outs[n_out[0]:]


def _rms_rows(x, gain_row):
    ms = jnp.mean(x * x, axis=-1, keepdims=True)
    return (x * lax.rsqrt(ms + NORM_EPS)) * gain_row


def _proj_chunk(h, w_ref, bd_ref, gain_ref, chunk):
    off, wd, gain_row = chunk
    parts = []
    for r0 in range(0, h.shape[0], ROW_TILE):
        acc = jnp.dot(h[r0:r0 + ROW_TILE], w_ref[:, off:off + wd], preferred_element_type=F32)
        if gain_row is not None:
            sq = (acc * acc).astype(BF16)
            step = min(wd, MXU_WIDTH)
            ss = jnp.concatenate(
                [jnp.dot(sq[:, c0:c0 + step], bd_ref[:step, :step], preferred_element_type=F32)
                 for c0 in range(0, wd, step)], axis=1)
            acc = (acc * lax.rsqrt(ss * (1.0 / HEAD_DIM) + NORM_EPS)) * gain_ref[gain_row:gain_row + 1, :wd]
        parts.append(acc)
    return parts[0] if len(parts) == 1 else jnp.concatenate(parts, axis=0)


def _sigmoid(z):
    return 1.0 / (1.0 + jnp.exp(-z))


def _prompt_proj_body(x_ref, ng_ref, w_ref, bd_ref, gain_ref, *refs, tt):
    outs = refs[:12]
    perm_ref = refs[12]
    h = _rms_rows(x_ref[0], ng_ref[...]).astype(BF16)
    for ci, chunk in enumerate(_CHUNKS):
        y = _proj_chunk(h, w_ref, bd_ref, gain_ref, chunk)
        mixer, role = divmod(ci, 3)
        o_ref = outs[ci]
        d = A_DILATIONS[mixer] if mixer < 3 else 1
        if mixer == 3:
            o_ref[0] = y.astype(BF16)
        elif d == 1:
            o_ref[0, 0] = y.astype(BF16)
        else:
            rows = tt // d
            for j in range(A_WIDTH // LANES):
                perm_ref[j] = y[:, j * LANES:(j + 1) * LANES]
            for r in range(d):
                for j in range(A_WIDTH // LANES):
                    o_ref[0, r, :, j * LANES:(j + 1) * LANES] = (
                        perm_ref[j, pl.ds(r, rows, stride=d), :].astype(BF16))


def _prompt_proj_job(x, ng, w_qkv, bd, gains):
    nb, seq, _ = x.shape
    tt = TOKEN_TILE
    tiles = seq // tt
    out_shape, out_specs = [], []
    for g in range(3):
        d = A_DILATIONS[g]
        for _ in range(3):
            out_shape.append(jax.ShapeDtypeStruct((nb, d, seq // d, A_WIDTH), BF16))
            out_specs.append(pl.BlockSpec((1, d, tt // d, A_WIDTH), lambda i: (i // tiles, 0, i % tiles, 0)))
    for wd in (A_WIDTH, B_KV_WIDTH, B_KV_WIDTH):
        out_shape.append(jax.ShapeDtypeStruct((nb, seq, wd), BF16))
        out_specs.append(pl.BlockSpec((1, tt, wd), lambda i: (i // tiles, i % tiles, 0)))
    return _Job(
        body=functools.partial(_prompt_proj_body, tt=tt),
        steps=nb * tiles,
        in_specs=[pl.BlockSpec((1, tt, D_MODEL), lambda i: (i // tiles, i % tiles, 0)),
                  _resident(ng.shape), _resident(w_qkv.shape), _resident(bd.shape), _resident(gains.shape)],
        out_specs=out_specs,
        out_shape=out_shape,
        scratch_shapes=[pltpu.VMEM((A_WIDTH // LANES, tt, LANES), F32)],
        args=(x, ng, w_qkv, bd, gains),
        name="prompt_proj")


_KV_ROWS = tuple((1024 * g + 512 * c, 512, c == 0) for g in range(3) for c in range(2)) + (
    (3072, 128, True), (3200, 128, False))


def _kv_transposed(h, wt_ref, gain_ref, rows):
    r0, nr, normed = rows
    y = lax.dot_general(wt_ref[r0:r0 + nr, :], h, (((1,), (1,)), ((), ())), preferred_element_type=F32)
    if normed:
        y3 = y.reshape(nr // HEAD_DIM, HEAD_DIM, y.shape[1])
        ms = jnp.mean(y3 * y3, axis=1, keepdims=True)
        y = (y3 * lax.rsqrt(ms + NORM_EPS)).reshape(y.shape) * gain_ref[r0:r0 + nr, :y.shape[1]]
    return y


def _prompt_cache_body(x_ref, ng_ref, wt_ref, gain_ref, c1_ref, c2_ref, c3_ref, cb_ref, *, tt):
    j = pl.program_id(1)
    last = pl.num_programs(1) - 1
    h = _rms_rows(x_ref[0], ng_ref[...]).astype(BF16)
    for c in range(2):
        c3_ref[0, c] = _kv_transposed(h, wt_ref, gain_ref, _KV_ROWS[4 + c])

    @pl.when(j == last)
    def _():
        tail = h[tt - BLOCK:]
        for c in range(2):
            c2_ref[0, c] = _kv_transposed(h, wt_ref, gain_ref, _KV_ROWS[2 + c])
            c1_ref[0, c] = _kv_transposed(tail, wt_ref, gain_ref, _KV_ROWS[c])
            cb_ref[0, c] = _kv_transposed(tail, wt_ref, gain_ref, _KV_ROWS[6 + c])


def _prompt_cache(x, ng, w_kv_t, gain_cols):
    nb, seq, _ = x.shape
    tt = CACHE_TILE
    assert A_WINDOWS == (BLOCK, tt, 4 * tt) and B_WINDOW == BLOCK
    n_tiles = A_WINDOWS[2] // tt
    first = seq // tt - n_tiles
    out_shape = [jax.ShapeDtypeStruct((nb, 2, A_WIDTH, w), F32) for w in A_WINDOWS]
    out_shape.append(jax.ShapeDtypeStruct((nb, 2, B_KV_WIDTH, B_WINDOW), F32))
    out_specs = [pl.BlockSpec((1, 2, A_WIDTH, BLOCK), lambda n, j: (n, 0, 0, 0)),
                 pl.BlockSpec((1, 2, A_WIDTH, tt), lambda n, j: (n, 0, 0, 0)),
                 pl.BlockSpec((1, 2, A_WIDTH, tt), lambda n, j: (n, 0, 0, j)),
                 pl.BlockSpec((1, 2, B_KV_WIDTH, BLOCK), lambda n, j: (n, 0, 0, 0))]
    return pl.pallas_call(
        functools.partial(_prompt_cache_body, tt=tt),
        grid=(nb, n_tiles),
        in_specs=[pl.BlockSpec((1, tt, D_MODEL), lambda n, j: (n, first + j, 0)),
                  _resident(ng.shape), _resident(w_kv_t.shape), _resident(gain_cols.shape)],
        out_specs=out_specs,
        out_shape=out_shape,
        compiler_params=_params(dimension_semantics=("arbitrary", "arbitrary")),
        name="prompt_cache",
    )(x, ng, w_kv_t, gain_cols)


def _stat_lane(pos):
    return pos if pos % 2 else HEAD_DIM + pos


def _softmax_block(qm, kk, bias, v_ext):
    s = lax.dot_general(qm, kk, (((1,), (1,)), ((), ())), preferred_element_type=F32) + bias
    m = jnp.max(s, axis=-1, keepdims=True)
    p = jnp.exp(s - m).astype(BF16)
    return m, jnp.dot(p, v_ext, preferred_element_type=F32)


def _attn_body(q_ref, kp_ref, kc_ref, vp_ref, vc_ref, bias_ref, o_ref, ml_ref,
               *, shared_kv, head_of_pos, per_seq):
    n_blocks = Q_ROWS // BLOCK
    first = jnp.minimum(pl.program_id(0) % per_seq, 1)
    lane = lax.broadcasted_iota(jnp.int32, (BLOCK, LANES), 1)
    lo = lane < HEAD_DIM
    lo_kv = lax.broadcasted_iota(jnp.int32, (BLOCK + Q_ROWS, LANES), 1) < HEAD_DIM
    one = jnp.ones((), BF16)
    m_tiles = [jnp.zeros((BLOCK, LANES), F32) for _ in range(n_blocks)]
    l_tiles = [jnp.ones((BLOCK, LANES), F32) for _ in range(n_blocks)]

    def load_kv(cols):
        k_all = jnp.concatenate([kp_ref[0, :, cols], kc_ref[0, :, cols]], axis=0)
        v_all = jnp.concatenate([vp_ref[0, :, cols], vc_ref[0, :, cols]], axis=0)
        return k_all, (jnp.where(lo_kv, v_all, one), jnp.where(lo_kv, one, v_all))

    if shared_kv:
        k_all, v_ext = load_kv(slice(0, LANES))
    for i in range(N_HEADS // 2):
        cols = slice(i * LANES, (i + 1) * LANES)
        if not shared_kv:
            k_all, v_ext = load_kv(cols)
        for b in range(n_blocks):
            sel = first if b == 0 else 1
            keys = slice(b * BLOCK, (b + 2) * BLOCK)
            rows = slice(b * BLOCK, (b + 1) * BLOCK)
            qp = q_ref[0, rows, cols]
            m_pair, pv_pair = [], []
            for hh in range(2):
                qm = jnp.where(lo if hh == 0 else ~lo, qp, jnp.zeros_like(qp))
                m, pv = _softmax_block(qm, k_all[keys], bias_ref[sel, head_of_pos[2 * i + hh]], v_ext[hh][keys])
                m_pair.append(m)
                pv_pair.append(pv)
            o_ref[0, rows, cols] = jnp.where(lo, pv_pair[0], pv_pair[1])
            for hh in range(2):
                at = lane == _stat_lane(2 * i + hh)
                m_tiles[b] = jnp.where(at, m_pair[hh], m_tiles[b])
                l_tiles[b] = jnp.where(at, pv_pair[hh], l_tiles[b])
    for b in range(n_blocks):
        ml_ref[0, b * BLOCK:(b + 1) * BLOCK, :LANES] = m_tiles[b]
        ml_ref[0, b * BLOCK:(b + 1) * BLOCK, LANES:] = l_tiles[b]


def _attention_job(q, k, v, bias, head_of_pos):
    ns, steps, _ = q.shape
    kv_width = k.shape[-1]
    ratio = Q_ROWS // BLOCK
    per_seq = steps // Q_ROWS
    here = lambda i: (i // per_seq, i % per_seq, 0)
    qspec = pl.BlockSpec((1, Q_ROWS, A_WIDTH), here)
    cur = pl.BlockSpec((1, Q_ROWS, kv_width), here)
    prev = pl.BlockSpec((1, BLOCK, kv_width),
                        lambda i: (i // per_seq, jnp.maximum((i % per_seq) * ratio - 1, 0), 0))
    return _Job(
        body=functools.partial(_attn_body, shared_kv=kv_width == LANES, head_of_pos=head_of_pos,
                               per_seq=per_seq),
        steps=ns * per_seq,
        in_specs=[qspec, prev, cur, prev, cur, _resident(bias.shape)],
        out_specs=[qspec, pl.BlockSpec((1, Q_ROWS, 2 * LANES), here)],
        out_shape=[jax.ShapeDtypeStruct((ns, steps, A_WIDTH), F32),
                   jax.ShapeDtypeStruct((ns, steps, 2 * LANES), F32)],
        scratch_shapes=[],
        args=(q, k, k, v, v, bias),
        name="attention")


def _head_expand(w, e_ref):
    return jnp.dot(w.astype(BF16), e_ref[...], preferred_element_type=F32)


def _merge_math(x, ng_ref, wg_ref, pa_ref, pb_ref, wo_ref, e_ref, sink_ref, o_groups, ml_groups, o_b, ml_b):
    ms = [ml[:, :LANES] for ml in ml_groups]
    ls = [ml[:, LANES:] for ml in ml_groups]
    big = jnp.maximum(jnp.maximum(ms[0], ms[1]), ms[2])
    es = [jnp.exp(m - big) for m in ms]
    inv = 1.0 / (es[0] * ls[0] + es[1] * ls[1] + es[2] * ls[2])
    o_a = None
    for e, o in zip(es, o_groups):
        term = _head_expand(e * inv, e_ref) * o
        o_a = term if o_a is None else o_a + term
    inv_b = 1.0 / (ml_b[:, LANES:] + jnp.exp(sink_ref[...] - ml_b[:, :LANES]))
    o_b = _head_expand(inv_b, e_ref) * o_b
    h = _rms_rows(x, ng_ref[...]).astype(BF16)
    gates = jnp.dot(h, wg_ref[...], preferred_element_type=F32)
    z_a, z_b = gates[:, :A_WIDTH], gates[:, A_WIDTH:2 * A_WIDTH]
    g_a, g_b = gates[:, 2 * A_WIDTH:2 * A_WIDTH + D_MODEL], gates[:, 2 * A_WIDTH + D_MODEL:]
    a = (o_a * (z_a * _sigmoid(z_a))).astype(BF16)
    b = (o_b * (z_b * _sigmoid(z_b))).astype(BF16)
    mixed = (_sigmoid(g_a) * jnp.dot(a, pa_ref[...], preferred_element_type=F32)
             + _sigmoid(g_b) * jnp.dot(b, pb_ref[...], preferred_element_type=F32))
    return x + jnp.dot(mixed.astype(BF16), wo_ref[...], preferred_element_type=F32)


def _prompt_merge_body(x_ref, ng_ref, wg_ref, pa_ref, pb_ref, wo_ref, e_ref, sink_ref,
                       o1_ref, l1_ref, o2_ref, l2_ref, o3_ref, l3_ref, ob_ref, lb_ref, y_ref,
                       o_scr, l_scr, *, tt):
    for gi, (o_ref, l_ref, d) in enumerate(((o2_ref, l2_ref, 4), (o3_ref, l3_ref, 16))):
        rows = tt // d
        for r in range(d):
            for j in range(2):
                l_scr[gi, j, pl.ds(r, rows, stride=d), :] = l_ref[0, r, :, j * LANES:(j + 1) * LANES]
            for j in range(A_WIDTH // LANES):
                o_scr[gi, j, pl.ds(r, rows, stride=d), :] = o_ref[0, r, :, j * LANES:(j + 1) * LANES]
    o_groups = [o1_ref[0, 0]] + [
        jnp.concatenate([o_scr[gi, j] for j in range(A_WIDTH // LANES)], axis=1) for gi in range(2)]
    ml_groups = [l1_ref[0, 0]] + [jnp.concatenate([l_scr[gi, 0], l_scr[gi, 1]], axis=1) for gi in range(2)]
    y_ref[0] = _merge_math(x_ref[0], ng_ref, wg_ref, pa_ref, pb_ref, wo_ref, e_ref, sink_ref,
                           o_groups, ml_groups, ob_ref[0], lb_ref[0])


def _prompt_merge_job(x, ng, w_gates, p_a, p_b, w_o, expand, sink_row, o_list, ml_list, o_b, ml_b):
    nb, seq, _ = x.shape
    tt = TOKEN_TILE
    tiles = seq // tt
    tile3 = lambda i: (i // tiles, i % tiles, 0)
    tile4 = lambda i: (i // tiles, 0, i % tiles, 0)
    in_specs = [pl.BlockSpec((1, tt, D_MODEL), tile3)]
    in_specs += [_resident(a.shape) for a in (ng, w_gates, p_a, p_b, w_o, expand, sink_row)]
    args = [x, ng, w_gates, p_a, p_b, w_o, expand, sink_row]
    for g in range(3):
        d = A_DILATIONS[g]
        in_specs.append(pl.BlockSpec((1, d, tt // d, A_WIDTH), tile4))
        in_specs.append(pl.BlockSpec((1, d, tt // d, 2 * LANES), tile4))
        args += [o_list[g].reshape(nb, d, seq // d, A_WIDTH), ml_list[g].reshape(nb, d, seq // d, 2 * LANES)]
    in_specs.append(pl.BlockSpec((1, tt, A_WIDTH), tile3))
    in_specs.append(pl.BlockSpec((1, tt, 2 * LANES), tile3))
    args += [o_b, ml_b]
    return _Job(
        body=functools.partial(_prompt_merge_body, tt=tt),
        steps=nb * tiles,
        in_specs=in_specs,
        out_specs=[pl.BlockSpec((1, tt, D_MODEL), tile3)],
        out_shape=[jax.ShapeDtypeStruct((nb, seq, D_MODEL), F32)],
        scratch_shapes=[pltpu.VMEM((2, A_WIDTH // LANES, tt, LANES), F32),
                        pltpu.VMEM((2, 2, tt, LANES), F32)],
        args=args,
        name="prompt_merge")


def _sample_proj_body(x_ref, ng_ref, w_ref, bd_ref, gain_ref, q_ref, kva_ref, kvb_ref, kvta_ref, kvtb_ref,
                      y_scr):
    dec = x_ref.shape[0]
    h = _rms_rows(x_ref[...], ng_ref[...]).astype(BF16)
    kvta_ref[...] = jnp.zeros_like(kvta_ref)
    kvtb_ref[...] = jnp.zeros_like(kvtb_ref)
    for ci, chunk in enumerate(_CHUNKS):
        wd = chunk[1]
        y = _proj_chunk(h, w_ref, bd_ref, gain_ref, chunk)
        mixer, role = divmod(ci, 3)
        if role == 0:
            q_ref[:, mixer * A_WIDTH:(mixer + 1) * A_WIDTH] = y
            continue
        y_scr[:, :wd] = y
        y_t = y_scr[:, :wd].T
        if mixer < 3:
            kva_ref[mixer, role - 1] = y
            kvta_ref[mixer, role - 1, :, :dec] = y_t
        else:
            kvb_ref[role - 1] = y
            kvtb_ref[role - 1, :, :dec] = y_t


def _sample_proj(x, ng, w_qkv, bd, gains):
    dec = x.shape[0]
    assert dec <= LANES
    out_shape = [jax.ShapeDtypeStruct((dec, 4 * A_WIDTH), F32),
                 jax.ShapeDtypeStruct((3, 2, dec, A_WIDTH), F32),
                 jax.ShapeDtypeStruct((2, dec, B_KV_WIDTH), F32),
                 jax.ShapeDtypeStruct((3, 2, A_WIDTH, LANES), F32),
                 jax.ShapeDtypeStruct((2, B_KV_WIDTH, LANES), F32)]
    args = (x, ng, w_qkv, bd, gains)
    return pl.pallas_call(
        _sample_proj_body,
        grid=(1,),
        in_specs=[_whole(a.shape) for a in args],
        out_specs=[_whole(o.shape) for o in out_shape],
        out_shape=out_shape,
        scratch_shapes=[pltpu.VMEM((dec, A_WIDTH), F32)],
        compiler_params=_params(dimension_semantics=("arbitrary",)),
        name="sample_proj",
    )(*args)


def _cache_stream_body(c_ref, q_ref, kn_ref, vn_ref, kvt_ref, bias_ref, out_ref, o_ref, ml_ref,
                       *, nb, kv_heads):
    window = c_ref.shape[-1]
    reps = N_HEADS // kv_heads
    step = pl.program_id(0)
    head_lane = lax.broadcasted_iota(jnp.int32, (N_HEADS, A_WIDTH), 1) // HEAD_DIM
    head_row = lax.broadcasted_iota(jnp.int32, (N_HEADS, A_WIDTH), 0)
    own = head_lane == head_row
    srow = lax.broadcasted_iota(jnp.int32, (N_HEADS, LANES), 0)
    slane = lax.broadcasted_iota(jnp.int32, (N_HEADS, LANES), 1)
    stat = slane == jnp.where(srow % 2 == 1, srow, srow + HEAD_DIM)
    used = jnp.sum(stat.astype(F32), axis=0, keepdims=True) > 0.0
    lane_w = lax.broadcasted_iota(jnp.int32, (HEAD_DIM, window), 1)

    def stacked(tiles):
        return jnp.concatenate([tiles[pos % kv_heads] for pos in range(N_HEADS)], axis=0)

    def tiled(row):
        return row if reps == 1 else jnp.concatenate([row] * reps, axis=1)

    for bi in range(nb):
        n = step * nb + bi
        q_row = q_ref[n]
        kn_row = tiled(kn_ref[n])
        vn_row = tiled(vn_ref[n])
        q_bd = jnp.where(own, q_row, 0.0)
        k_all = stacked([c_ref[bi, 0, g].astype(BF16) for g in range(kv_heads)])
        v_all = stacked([c_ref[bi, 1, g].astype(BF16) for g in range(kv_heads)])
        s = jnp.dot(q_bd.astype(BF16), k_all, preferred_element_type=F32) + bias_ref[...]
        s_new = jnp.sum(q_bd * kn_row, axis=1, keepdims=True)
        m = jnp.maximum(jnp.max(s, axis=1, keepdims=True), s_new)
        p = jnp.exp(s - m)
        p_new = jnp.exp(s_new - m)
        l = jnp.sum(p, axis=1, keepdims=True) + p_new
        o_t = lax.dot_general(p.astype(BF16), v_all, (((1,), (1,)), ((), ())), preferred_element_type=F32)
        o_ref[n] = jnp.sum(jnp.where(own, o_t + p_new * vn_row, 0.0), axis=0, keepdims=True)
        ml_ref[n, :, :LANES] = jnp.sum(jnp.where(stat, m, 0.0), axis=0, keepdims=True)
        ml_ref[n, :, LANES:] = jnp.where(used, jnp.sum(jnp.where(stat, l, 0.0), axis=0, keepdims=True), 1.0)
        for c in range(2):
            for g in range(kv_heads):
                new = pltpu.roll(kvt_ref[c, g * HEAD_DIM:(g + 1) * HEAD_DIM, :], LANES - 1 - n, axis=1)
                if window > LANES:
                    new = jnp.concatenate([new] * (window // LANES), axis=1)
                shifted = pltpu.roll(c_ref[bi, c, g], window - 1, axis=1)
                out_ref[bi, c, g] = jnp.where(lane_w == window - 1, new, shifted)


def _cache_stream_job(cache_t, q, mixer, kv_new, kv_new_t, bias_rows, nb):
    dec, _, kv_heads, _, window = cache_t.shape
    nb = min(nb, dec)
    blk = (nb, 2, kv_heads, HEAD_DIM, window)
    kv_width = kv_heads * HEAD_DIM
    return _Job(
        body=functools.partial(_cache_stream_body, nb=nb, kv_heads=kv_heads),
        steps=dec // nb,
        in_specs=[pl.BlockSpec(blk, lambda i: (i, 0, 0, 0, 0)),
                  pl.BlockSpec((dec, 1, A_WIDTH), lambda i: (0, 0, mixer)),
                  pl.BlockSpec((None, dec, 1, kv_width), lambda i: (0, 0, 0, 0)),
                  pl.BlockSpec((None, dec, 1, kv_width), lambda i: (1, 0, 0, 0)),
                  _resident(kv_new_t.shape), _resident(bias_rows.shape)],
        out_specs=[pl.BlockSpec(blk, lambda i: (i, 0, 0, 0, 0)),
                   pl.BlockSpec((dec, 1, A_WIDTH), lambda i: (0, 0, 0)),
                   pl.BlockSpec((dec, 1, 2 * LANES), lambda i: (0, 0, 0))],
        out_shape=[jax.ShapeDtypeStruct(cache_t.shape, F32),
                   jax.ShapeDtypeStruct((dec, 1, A_WIDTH), F32),
                   jax.ShapeDtypeStruct((dec, 1, 2 * LANES), F32)],
        scratch_shapes=[],
        args=(cache_t, q, kv_new, kv_new, kv_new_t, bias_rows),
        name="cache_stream")


def _sample_merge_body(x_ref, ng_ref, wg_ref, pa_ref, pb_ref, wo_ref, e_ref, sink_ref,
                       o1_ref, l1_ref, o2_ref, l2_ref, o3_ref, l3_ref, ob_ref, lb_ref, y_ref):
    y_ref[...] = _merge_math(x_ref[...], ng_ref, wg_ref, pa_ref, pb_ref, wo_ref, e_ref, sink_ref,
                             [o1_ref[...], o2_ref[...], o3_ref[...]],
                             [l1_ref[...], l2_ref[...], l3_ref[...]], ob_ref[...], lb_ref[...])


def _sample_merge(x, ng, w_gates, p_a, p_b, w_o, expand, sink_row, o_list, ml_list):
    args = [x, ng, w_gates, p_a, p_b, w_o, expand, sink_row]
    for o, l in zip(o_list, ml_list):
        args += [o, l]
    return pl.pallas_call(
        _sample_merge_body,
        grid=(1,),
        in_specs=[_whole(a.shape) for a in args],
        out_specs=_whole(x.shape),
        out_shape=jax.ShapeDtypeStruct(x.shape, F32),
        compiler_params=_params(dimension_semantics=("arbitrary",)),
        name="sample_merge",
    )(*args)


def _buffer_bias_rows(window, dilation):
    r = np.arange(window)
    steps = (window - r) // dilation
    slopes = np.exp2(-np.arange(1, N_HEADS + 1, dtype=np.float32))
    bias = -slopes[:, None] * steps[None].astype(np.float32)
    return np.where((r % dilation == 0)[None], bias, np.float32(NEG)).astype(np.float32)


def _band_bias_table():
    q = np.arange(BLOCK)[:, None]
    kk = np.arange(2 * BLOCK)[None, :]
    delta = q + BLOCK - kk
    valid = (delta >= 0) & (delta <= BLOCK)
    slopes = np.exp2(-np.arange(1, N_HEADS + 1, dtype=np.float32))
    bias = -slopes[:, None, None] * delta[None].astype(np.float32)
    normal = np.where(valid[None], bias, np.float32(NEG))
    first = np.where((valid & (kk >= BLOCK))[None], bias, np.float32(NEG))
    return np.stack([first, normal]).astype(np.float32)


def _block_diag_ones():
    i = np.arange(MXU_WIDTH)
    return (i[:, None] // HEAD_DIM == i[None, :] // HEAD_DIM).astype(np.float32)


def _head_expand_matrix():
    e = np.zeros((LANES, A_WIDTH), np.float32)
    for pos in range(N_HEADS):
        e[_stat_lane(pos), pos * HEAD_DIM:(pos + 1) * HEAD_DIM] = 1.0
    return e


def _b_column_order():
    return np.concatenate([np.arange(HEAD_DIM) + HEAD_DIM * hd for hd in B_HEAD_ORDER])


def _prepare_weights(norm_gain, w_in, qk_norm_a, qk_norm_b, b_sinks, w_branch_a, w_branch_b, w_out):
    w = w_in[0]
    order = _b_column_order()
    b_q0 = 9 * A_WIDTH
    w_qkv = jnp.concatenate([w[:, :b_q0], w[:, b_q0:b_q0 + A_WIDTH][:, order],
                             w[:, b_q0 + A_WIDTH:QKV_WIDTH]], axis=1).astype(BF16)
    g0 = QKV_WIDTH
    w_gates = jnp.concatenate([w[:, g0:g0 + A_WIDTH], w[:, g0 + A_WIDTH:g0 + 2 * A_WIDTH][:, order],
                               w[:, g0 + 2 * A_WIDTH:]], axis=1).astype(BF16)
    scale = HEAD_DIM ** -0.5
    rows = []
    for g in range(3):
        rows.append(jnp.tile(qk_norm_a[0, g, 0], N_HEADS) * scale)
        rows.append(jnp.tile(qk_norm_a[0, g, 1], N_HEADS))
    rows.append(jnp.tile(qk_norm_b[0, 0], N_HEADS) * scale)
    rows.append(jnp.tile(qk_norm_b[0, 1], N_HEADS))
    gains = jnp.stack(rows).astype(F32)
    lanes = np.array([_stat_lane(pos) for pos in range(N_HEADS)])
    sink_row = jnp.zeros((1, LANES), F32).at[0, lanes].set(b_sinks[0][np.array(B_HEAD_ORDER)].astype(F32))
    kv_cols = np.concatenate([np.arange(1536 * g + 512, 1536 * g + 1536) for g in range(3)]
                             + [np.arange(5120, 5376)])
    w_kv_t = w[:, kv_cols].T.astype(BF16)
    ones = jnp.ones((A_WIDTH,), F32)
    gain_rows = [r for g in range(3) for r in (jnp.tile(qk_norm_a[0, g, 1], N_HEADS), ones)]
    gain_rows += [jnp.tile(qk_norm_b[0, 1], B_KV_HEADS), ones[:B_KV_WIDTH]]
    gain_cols = jnp.broadcast_to(jnp.concatenate(gain_rows).astype(F32)[:, None], (kv_cols.size, CACHE_TILE))
    return dict(ng=norm_gain.astype(F32), w_qkv=w_qkv, w_gates=w_gates, gains=gains, sink_row=sink_row,
                w_kv_t=w_kv_t, gain_cols=gain_cols,
                p_a=w_branch_a[0].astype(BF16), p_b=w_branch_b[0][order].astype(BF16),
                w_o=w_out[0].astype(BF16))


def _cache_out(t):
    nb, two, width, rows = t.shape
    t = t.reshape(nb, two, width // HEAD_DIM, HEAD_DIM, rows)
    return jnp.transpose(t, (0, 4, 1, 2, 3))[None]


STREAM_BATCH = (2, 1, 1, 2)


def _layer(x_prompt, x_sample, caches, wts, consts):
    nb, seq, _ = x_prompt.shape
    dec = x_sample.shape[0]
    x2 = x_sample.reshape(dec, D_MODEL)
    q_s, kv_a, kv_b, kvt_a, kvt_b = _sample_proj(x2, wts["ng"], wts["w_qkv"], consts["bd"], wts["gains"])

    def stream_job(mixer):
        cache_t = jnp.transpose(caches[mixer][0], (0, 2, 3, 4, 1))
        kv_new, kv_new_t = (kv_b, kvt_b) if mixer == 3 else (kv_a[mixer], kvt_a[mixer])
        return _cache_stream_job(cache_t, q_s[:, None, :], mixer, kv_new[:, :, None, :], kv_new_t,
                                 consts["rows"][mixer], STREAM_BATCH[mixer])

    streams = [None] * 4
    qkv, streams[2] = _run_pair(
        _prompt_proj_job(x_prompt, wts["ng"], wts["w_qkv"], consts["bd"], wts["gains"]), stream_job(2))
    prompt_caches = [_cache_out(c) for c in _prompt_cache(x_prompt, wts["ng"], wts["w_kv_t"], wts["gain_cols"])]
    def attention_job(g):
        d = A_DILATIONS[g]
        shape = (nb * d, seq // d, A_WIDTH)
        return _attention_job(*(a.reshape(shape) for a in qkv[3 * g:3 * g + 3]), consts["bias"], NATURAL_ORDER)

    attn = [None] * 3
    attn[0], streams[0] = _run_pair(attention_job(0), stream_job(0))
    attn[1] = _run(attention_job(1))
    attn[2] = _run(attention_job(2))
    (o_b, ml_b), streams[3] = _run_pair(
        _attention_job(qkv[9], qkv[10], qkv[11], consts["bias"], B_HEAD_ORDER), stream_job(3))
    o_list, ml_list = [a[0] for a in attn], [a[1] for a in attn]
    (y_prompt,), streams[1] = _run_pair(
        _prompt_merge_job(x_prompt, wts["ng"], wts["w_gates"], wts["p_a"], wts["p_b"], wts["w_o"],
                          consts["expand"], wts["sink_row"], o_list, ml_list, o_b, ml_b), stream_job(1))
    sample_caches = [jnp.transpose(s[0], (0, 4, 1, 2, 3))[None] for s in streams]
    y_sample = _sample_merge(x2, wts["ng"], wts["w_gates"], wts["p_a"], wts["p_b"], wts["w_o"],
                             consts["expand"], wts["sink_row"],
                             [s[1].reshape(dec, A_WIDTH) for s in streams],
                             [s[2].reshape(dec, 2 * LANES) for s in streams])
    return y_prompt, y_sample.reshape(dec, 1, D_MODEL), prompt_caches, sample_caches


def _constants():
    assert all(B_HEAD_ORDER[pos] // (N_HEADS // B_KV_HEADS) == pos % B_KV_HEADS for pos in range(N_HEADS))
    rows = [jnp.asarray(_buffer_bias_rows(w, d)) for w, d in zip(A_WINDOWS, A_DILATIONS)]
    rows.append(jnp.asarray(_buffer_bias_rows(B_WINDOW, 1)[np.array(B_HEAD_ORDER)]))
    return dict(bias=jnp.asarray(_band_bias_table()), bd=jnp.asarray(_block_diag_ones(), BF16),
                expand=jnp.asarray(_head_expand_matrix(), BF16), rows=rows)


def kernel(x_prompt, x_sample, cache_a1_kv, cache_a2_kv, cache_a3_kv, cache_b_kv, norm_gain, w_in,
           qk_norm_a, qk_norm_b, b_sinks, w_branch_a, w_branch_b, w_out):
    wts = _prepare_weights(norm_gain, w_in, qk_norm_a, qk_norm_b, b_sinks, w_branch_a, w_branch_b, w_out)
    consts = _constants()
    y_prompt, y_sample, pc, sc = _layer(x_prompt, x_sample, (cache_a1_kv, cache_a2_kv, cache_a3_kv, cache_b_kv),
                                        wts, consts)
    return (y_prompt, y_sample, pc[0], pc[1], pc[2], pc[3], sc[0], sc[1], sc[2], sc[3])
```

```python
import functools
from typing import Callable, NamedTuple, Optional, Sequence

import numpy as np
import jax
import jax.numpy as jnp
from jax import lax
from jax.experimental import pallas as pl
from jax.experimental.pallas import tpu as pltpu

F32 = jnp.float32
BF16 = jnp.bfloat16

D_MODEL = 1024
HEAD_DIM = 64
N_HEADS = 8
A_WINDOWS = (128, 512, 2048)
A_DILATIONS = (1, 4, 16)
B_WINDOW = 128
B_KV_HEADS = 2
A_WIDTH = N_HEADS * HEAD_DIM
B_KV_WIDTH = B_KV_HEADS * HEAD_DIM
QKV_WIDTH = 9 * A_WIDTH + A_WIDTH + 2 * B_KV_WIDTH
BLOCK = 128
Q_ROWS = 512
NORM_EPS = 1e-6
NEG = -1e30
LOG2E = float(np.log2(np.e))
LANES = 128
MXU_WIDTH = 256
TOKEN_TILE = 256
CACHE_TILE = 512
ROW_TILE = 256
VMEM_LIMIT = 56 * 1024 * 1024
B_HEAD_ORDER = (0, 4, 1, 5, 2, 6, 3, 7)
NATURAL_ORDER = tuple(range(N_HEADS))

_CHUNKS = []
for _g in range(3):
    _CHUNKS += [(1536 * _g, 512, 2 * _g), (1536 * _g + 512, 512, 2 * _g + 1), (1536 * _g + 1024, 512, None)]
_CHUNKS += [(4608, 512, 6), (5120, 128, 7), (5248, 128, None)]


def _params(**kw):
    return pltpu.CompilerParams(vmem_limit_bytes=VMEM_LIMIT, **kw)


def _resident(shape):
    nd = len(shape)
    return pl.BlockSpec(shape, lambda *_: (0,) * nd, pipeline_mode=pl.Buffered(1))


def _whole(shape):
    nd = len(shape)
    return pl.BlockSpec(shape, lambda *_: (0,) * nd)


class _Job(NamedTuple):
    body: Callable
    steps: int
    in_specs: Sequence
    out_specs: Sequence
    out_shape: Sequence
    scratch_shapes: Sequence
    args: Sequence
    name: str
    aliases: Optional[dict] = None


def _run(job):
    return pl.pallas_call(
        job.body, grid=(job.steps,), in_specs=list(job.in_specs), out_specs=list(job.out_specs),
        out_shape=list(job.out_shape), scratch_shapes=list(job.scratch_shapes),
        input_output_aliases=dict(job.aliases or {}),
        compiler_params=_params(dimension_semantics=("arbitrary",)), name=job.name)(*job.args)


def _run_pair(a, b):
    if a.steps != b.steps:
        return _run(a), _run(b)
    n_in = (len(a.in_specs), len(b.in_specs))
    n_out = (len(a.out_specs), len(b.out_specs))
    n_scr = len(a.scratch_shapes)

    def body(*refs):
        ins, rest = refs[:sum(n_in)], refs[sum(n_in):]
        outs, scr = rest[:sum(n_out)], rest[sum(n_out):]
        a.body(*ins[:n_in[0]], *outs[:n_out[0]], *scr[:n_scr])
        b.body(*ins[n_in[0]:], *outs[n_out[0]:], *scr[n_scr:])

    aliases = dict(a.aliases or {})
    aliases.update({n_in[0] + i: n_out[0] + o for i, o in (b.aliases or {}).items()})
    outs = pl.pallas_call(
        body, grid=(a.steps,), in_specs=list(a.in_specs) + list(b.in_specs),
        out_specs=list(a.out_specs) + list(b.out_specs), out_shape=list(a.out_shape) + list(b.out_shape),
        scratch_shapes=list(a.scratch_shapes) + list(b.scratch_shapes), input_output_aliases=aliases,
        compiler_params=_params(dimension_semantics=("arbitrary",)), name=a.name + "_" + b.name,
    )(*a.args, *b.args)
    return outs[:n_out[0]], outs[n_out[0]:]


def _rms_rows(x, gain_row):
    ms = jnp.mean(x * x, axis=-1, keepdims=True)
    return (x * lax.rsqrt(ms + NORM_EPS)) * gain_row


def _proj_chunk(h, w_ref, bd_ref, gain_ref, chunk):
    off, wd, gain_row = chunk
    parts = []
    for r0 in range(0, h.shape[0], ROW_TILE):
        acc = jnp.dot(h[r0:r0 + ROW_TILE], w_ref[:, off:off + wd], preferred_element_type=F32)
        if gain_row is not None:
            sq = (acc * acc).astype(BF16)
            step = min(wd, MXU_WIDTH)
            ss = jnp.concatenate(
                [jnp.dot(sq[:, c0:c0 + step], bd_ref[:step, :step], preferred_element_type=F32)
                 for c0 in range(0, wd, step)], axis=1)
            acc = (acc * lax.rsqrt(ss * (1.0 / HEAD_DIM) + NORM_EPS)) * gain_ref[gain_row:gain_row + 1, :wd]
        parts.append(acc)
    return parts[0] if len(parts) == 1 else jnp.concatenate(parts, axis=0)


def _sigmoid(z):
    return 1.0 / (1.0 + jnp.exp(-z))


def _prompt_proj_body(x_ref, ng_ref, w_ref, bd_ref, gain_ref, *refs, tt):
    outs = refs[:12]
    perm_ref = refs[12]
    h = _rms_rows(x_ref[0], ng_ref[...]).astype(BF16)
    for ci, chunk in enumerate(_CHUNKS):
        y = _proj_chunk(h, w_ref, bd_ref, gain_ref, chunk)
        mixer, role = divmod(ci, 3)
        o_ref = outs[ci]
        d = A_DILATIONS[mixer] if mixer < 3 else 1
        if mixer == 3:
            o_ref[0] = y.astype(BF16)
        elif d == 1:
            o_ref[0, 0] = y.astype(BF16)
        else:
            rows = tt // d
            for j in range(A_WIDTH // LANES):
                perm_ref[j] = y[:, j * LANES:(j + 1) * LANES]
            for r in range(d):
                for j in range(A_WIDTH // LANES):
                    o_ref[0, r, :, j * LANES:(j + 1) * LANES] = (
                        perm_ref[j, pl.ds(r, rows, stride=d), :].astype(BF16))


def _prompt_proj_job(x, ng, w_qkv, bd, gains):
    nb, seq, _ = x.shape
    tt = TOKEN_TILE
    tiles = seq // tt
    out_shape, out_specs = [], []
    for g in range(3):
        d = A_DILATIONS[g]
        for _ in range(3):
            out_shape.append(jax.ShapeDtypeStruct((nb, d, seq // d, A_WIDTH), BF16))
            out_specs.append(pl.BlockSpec((1, d, tt // d, A_WIDTH), lambda i: (i // tiles, 0, i % tiles, 0)))
    for wd in (A_WIDTH, B_KV_WIDTH, B_KV_WIDTH):
        out_shape.append(jax.ShapeDtypeStruct((nb, seq, wd), BF16))
        out_specs.append(pl.BlockSpec((1, tt, wd), lambda i: (i // tiles, i % tiles, 0)))
    return _Job(
        body=functools.partial(_prompt_proj_body, tt=tt),
        steps=nb * tiles,
        in_specs=[pl.BlockSpec((1, tt, D_MODEL), lambda i: (i // tiles, i % tiles, 0)),
                  _resident(ng.shape), _resident(w_qkv.shape), _resident(bd.shape), _resident(gains.shape)],
        out_specs=out_specs,
        out_shape=out_shape,
        scratch_shapes=[pltpu.VMEM((A_WIDTH // LANES, tt, LANES), F32)],
        args=(x, ng, w_qkv, bd, gains),
        name="prompt_proj")


def _prompt_cache_body(x_ref, ng_ref, w_ref, bd_ref, gain_ref, c1_ref, c2_ref, c3_ref, cb_ref, y_scr, *, tt):
    j = pl.program_id(1)
    last = pl.num_programs(1) - 1
    h = _rms_rows(x_ref[0], ng_ref[...]).astype(BF16)

    def kv_t(rows, mixer, c):
        wd = _CHUNKS[3 * mixer + 1 + c][1]
        y_scr[:rows.shape[0], :wd] = _proj_chunk(rows, w_ref, bd_ref, gain_ref, _CHUNKS[3 * mixer + 1 + c])
        return y_scr[:rows.shape[0], :wd].T

    for c in range(2):
        c3_ref[0, c] = kv_t(h, 2, c)

    @pl.when(j == last)
    def _():
        tail = h[tt - BLOCK:]
        for c in range(2):
            c2_ref[0, c] = kv_t(h, 1, c)
            c1_ref[0, c] = kv_t(tail, 0, c)
            cb_ref[0, c] = kv_t(tail, 3, c)


def _prompt_cache(x, ng, w_qkv, bd, gains):
    nb, seq, _ = x.shape
    tt = CACHE_TILE
    assert A_WINDOWS == (BLOCK, tt, 4 * tt) and B_WINDOW == BLOCK
    n_tiles = A_WINDOWS[2] // tt
    first = seq // tt - n_tiles
    out_shape = [jax.ShapeDtypeStruct((nb, 2, A_WIDTH, w), F32) for w in A_WINDOWS]
    out_shape.append(jax.ShapeDtypeStruct((nb, 2, B_KV_WIDTH, B_WINDOW), F32))
    out_specs = [pl.BlockSpec((1, 2, A_WIDTH, BLOCK), lambda n, j: (n, 0, 0, 0)),
                 pl.BlockSpec((1, 2, A_WIDTH, tt), lambda n, j: (n, 0, 0, 0)),
                 pl.BlockSpec((1, 2, A_WIDTH, tt), lambda n, j: (n, 0, 0, j)),
                 pl.BlockSpec((1, 2, B_KV_WIDTH, BLOCK), lambda n, j: (n, 0, 0, 0))]
    return pl.pallas_call(
        functools.partial(_prompt_cache_body, tt=tt),
        grid=(nb, n_tiles),
        in_specs=[pl.BlockSpec((1, tt, D_MODEL), lambda n, j: (n, first + j, 0)),
                  _resident(ng.shape), _resident(w_qkv.shape), _resident(bd.shape), _resident(gains.shape)],
        out_specs=out_specs,
        out_shape=out_shape,
        scratch_shapes=[pltpu.VMEM((tt, A_WIDTH), F32)],
        compiler_params=_params(dimension_semantics=("arbitrary", "arbitrary")),
        name="prompt_cache",
    )(x, ng, w_qkv, bd, gains)


def _stat_lane(pos):
    return pos if pos % 2 else HEAD_DIM + pos


def _softmax_block(qm, kk, bias, v_ext):
    s = lax.dot_general(qm, kk, (((1,), (1,)), ((), ())), preferred_element_type=F32) + bias
    m = jnp.max(s, axis=-1, keepdims=True)
    p = jnp.exp2(s - m).astype(BF16)
    return m, jnp.dot(p, v_ext, preferred_element_type=F32)


def _attn_body(q_ref, kp_ref, kc_ref, vp_ref, vc_ref, bias_ref, o_ref, ml_ref,
               *, shared_kv, head_of_pos, per_seq):
    n_blocks = Q_ROWS // BLOCK
    first = jnp.minimum(pl.program_id(0) % per_seq, 1)
    lane = lax.broadcasted_iota(jnp.int32, (BLOCK, LANES), 1)
    lo = lane < HEAD_DIM
    lo_kv = lax.broadcasted_iota(jnp.int32, (BLOCK + Q_ROWS, LANES), 1) < HEAD_DIM
    one = jnp.ones((), BF16)
    ml_ref[0, :, :LANES] = jnp.zeros((Q_ROWS, LANES), F32)
    ml_ref[0, :, LANES:] = jnp.ones((Q_ROWS, LANES), F32)

    def load_kv(cols):
        k_all = jnp.concatenate([kp_ref[0, :, cols], kc_ref[0, :, cols]], axis=0)
        v_all = jnp.concatenate([vp_ref[0, :, cols], vc_ref[0, :, cols]], axis=0)
        return k_all, (jnp.where(lo_kv, v_all, one), jnp.where(lo_kv, one, v_all))

    if shared_kv:
        k_all, v_ext = load_kv(slice(0, LANES))
    for i in range(N_HEADS // 2):
        cols = slice(i * LANES, (i + 1) * LANES)
        if not shared_kv:
            k_all, v_ext = load_kv(cols)
        for b in range(n_blocks):
            sel = first if b == 0 else 1
            keys = slice(b * BLOCK, (b + 2) * BLOCK)
            rows = slice(b * BLOCK, (b + 1) * BLOCK)
            qp = q_ref[0, rows, cols]
            m_pair, pv_pair = [], []
            for hh in range(2):
                qm = jnp.where(lo if hh == 0 else ~lo, qp, jnp.zeros_like(qp))
                m, pv = _softmax_block(qm, k_all[keys], bias_ref[sel, head_of_pos[2 * i + hh]], v_ext[hh][keys])
                m_pair.append(m)
                pv_pair.append(pv)
            o_ref[0, rows, cols] = jnp.where(lo, pv_pair[0], pv_pair[1])
            for hh in range(2):
                at = _stat_lane(2 * i + hh)
                ml_ref[0, rows, at:at + 1] = m_pair[hh]
                ml_ref[0, rows, LANES + at:LANES + at + 1] = pv_pair[hh][:, at:at + 1]


def _attention_job(q, k, v, bias, head_of_pos):
    ns, steps, _ = q.shape
    kv_width = k.shape[-1]
    ratio = Q_ROWS // BLOCK
    per_seq = steps // Q_ROWS
    here = lambda i: (i // per_seq, i % per_seq, 0)
    qspec = pl.BlockSpec((1, Q_ROWS, A_WIDTH), here)
    cur = pl.BlockSpec((1, Q_ROWS, kv_width), here)
    prev = pl.BlockSpec((1, BLOCK, kv_width),
                        lambda i: (i // per_seq, jnp.maximum((i % per_seq) * ratio - 1, 0), 0))
    return _Job(
        body=functools.partial(_attn_body, shared_kv=kv_width == LANES, head_of_pos=head_of_pos,
                               per_seq=per_seq),
        steps=ns * per_seq,
        in_specs=[qspec, prev, cur, prev, cur, _resident(bias.shape)],
        out_specs=[qspec, pl.BlockSpec((1, Q_ROWS, 2 * LANES), here)],
        out_shape=[jax.ShapeDtypeStruct((ns, steps, A_WIDTH), F32),
                   jax.ShapeDtypeStruct((ns, steps, 2 * LANES), F32)],
        scratch_shapes=[],
        args=(q, k, k, v, v, bias),
        name="attention")


def _head_expand(w, e_ref):
    return jnp.dot(w.astype(BF16), e_ref[...], preferred_element_type=F32)


def _merge_math(x, ng_ref, wg_ref, pa_ref, pb_ref, wo_ref, e_ref, sink_ref, o_groups, ml_groups, o_b, ml_b):
    ms = [ml[:, :LANES] for ml in ml_groups]
    ls = [ml[:, LANES:] for ml in ml_groups]
    big = jnp.maximum(jnp.maximum(ms[0], ms[1]), ms[2])
    es = [jnp.exp2(m - big) for m in ms]
    inv = 1.0 / (es[0] * ls[0] + es[1] * ls[1] + es[2] * ls[2])
    o_a = None
    for e, o in zip(es, o_groups):
        term = _head_expand(e * inv, e_ref) * o
        o_a = term if o_a is None else o_a + term
    inv_b = 1.0 / (ml_b[:, LANES:] + jnp.exp2(sink_ref[...] - ml_b[:, :LANES]))
    o_b = _head_expand(inv_b, e_ref) * o_b
    h = _rms_rows(x, ng_ref[...]).astype(BF16)
    gates = jnp.dot(h, wg_ref[...], preferred_element_type=F32)
    z_a, z_b = gates[:, :A_WIDTH], gates[:, A_WIDTH:2 * A_WIDTH]
    g_a, g_b = gates[:, 2 * A_WIDTH:2 * A_WIDTH + D_MODEL], gates[:, 2 * A_WIDTH + D_MODEL:]
    a = (o_a * (z_a * _sigmoid(z_a))).astype(BF16)
    b = (o_b * (z_b * _sigmoid(z_b))).astype(BF16)
    mixed = (_sigmoid(g_a) * jnp.dot(a, pa_ref[...], preferred_element_type=F32)
             + _sigmoid(g_b) * jnp.dot(b, pb_ref[...], preferred_element_type=F32))
    return x + jnp.dot(mixed.astype(BF16), wo_ref[...], preferred_element_type=F32)


def _prompt_merge_body(x_ref, ng_ref, wg_ref, pa_ref, pb_ref, wo_ref, e_ref, sink_ref,
                       o1_ref, l1_ref, o2_ref, l2_ref, o3_ref, l3_ref, ob_ref, lb_ref, y_ref,
                       o_scr, l_scr, *, tt):
    for gi, (o_ref, l_ref, d) in enumerate(((o2_ref, l2_ref, 4), (o3_ref, l3_ref, 16))):
        rows = tt // d
        for r in range(d):
            for j in range(2):
                l_scr[gi, j, pl.ds(r, rows, stride=d), :] = l_ref[0, r, :, j * LANES:(j + 1) * LANES]
            for j in range(A_WIDTH // LANES):
                o_scr[gi, j, pl.ds(r, rows, stride=d), :] = o_ref[0, r, :, j * LANES:(j + 1) * LANES]
    o_groups = [o1_ref[0, 0]] + [
        jnp.concatenate([o_scr[gi, j] for j in range(A_WIDTH // LANES)], axis=1) for gi in range(2)]
    ml_groups = [l1_ref[0, 0]] + [jnp.concatenate([l_scr[gi, 0], l_scr[gi, 1]], axis=1) for gi in range(2)]
    y_ref[0] = _merge_math(x_ref[0], ng_ref, wg_ref, pa_ref, pb_ref, wo_ref, e_ref, sink_ref,
                           o_groups, ml_groups, ob_ref[0], lb_ref[0])


def _prompt_merge_job(x, ng, w_gates, p_a, p_b, w_o, expand, sink_row, o_list, ml_list, o_b, ml_b):
    nb, seq, _ = x.shape
    tt = TOKEN_TILE
    tiles = seq // tt
    tile3 = lambda i: (i // tiles, i % tiles, 0)
    tile4 = lambda i: (i // tiles, 0, i % tiles, 0)
    in_specs = [pl.BlockSpec((1, tt, D_MODEL), tile3)]
    in_specs += [_resident(a.shape) for a in (ng, w_gates, p_a, p_b, w_o, expand, sink_row)]
    args = [x, ng, w_gates, p_a, p_b, w_o, expand, sink_row]
    for g in range(3):
        d = A_DILATIONS[g]
        in_specs.append(pl.BlockSpec((1, d, tt // d, A_WIDTH), tile4))
        in_specs.append(pl.BlockSpec((1, d, tt // d, 2 * LANES), tile4))
        args += [o_list[g].reshape(nb, d, seq // d, A_WIDTH), ml_list[g].reshape(nb, d, seq // d, 2 * LANES)]
    in_specs.append(pl.BlockSpec((1, tt, A_WIDTH), tile3))
    in_specs.append(pl.BlockSpec((1, tt, 2 * LANES), tile3))
    args += [o_b, ml_b]
    return _Job(
        body=functools.partial(_prompt_merge_body, tt=tt),
        steps=nb * tiles,
        in_specs=in_specs,
        out_specs=[pl.BlockSpec((1, tt, D_MODEL), tile3)],
        out_shape=[jax.ShapeDtypeStruct((nb, seq, D_MODEL), F32)],
        scratch_shapes=[pltpu.VMEM((2, A_WIDTH // LANES, tt, LANES), F32),
                        pltpu.VMEM((2, 2, tt, LANES), F32)],
        args=args,
        name="prompt_merge")


def _sample_proj_body(x_ref, ng_ref, w_ref, bd_ref, gain_ref, q_ref, kva_ref, kvb_ref, kvta_ref, kvtb_ref,
                      y_scr):
    dec = x_ref.shape[0]
    h = _rms_rows(x_ref[...], ng_ref[...]).astype(BF16)
    kvta_ref[...] = jnp.zeros_like(kvta_ref)
    kvtb_ref[...] = jnp.zeros_like(kvtb_ref)
    for ci, chunk in enumerate(_CHUNKS):
        wd = chunk[1]
        y = _proj_chunk(h, w_ref, bd_ref, gain_ref, chunk)
        mixer, role = divmod(ci, 3)
        if role == 0:
            q_ref[:, mixer * A_WIDTH:(mixer + 1) * A_WIDTH] = y
            continue
        y_scr[:, :wd] = y
        y_t = y_scr[:, :wd].T
        if mixer < 3:
            kva_ref[mixer, role - 1] = y
            kvta_ref[mixer, role - 1, :, :dec] = y_t
        else:
            kvb_ref[role - 1] = y
            kvtb_ref[role - 1, :, :dec] = y_t


def _sample_proj(x, ng, w_qkv, bd, gains):
    dec = x.shape[0]
    assert dec <= LANES
    out_shape = [jax.ShapeDtypeStruct((dec, 4 * A_WIDTH), F32),
                 jax.ShapeDtypeStruct((3, 2, dec, A_WIDTH), F32),
                 jax.ShapeDtypeStruct((2, dec, B_KV_WIDTH), F32),
                 jax.ShapeDtypeStruct((3, 2, A_WIDTH, LANES), F32),
                 jax.ShapeDtypeStruct((2, B_KV_WIDTH, LANES), F32)]
    args = (x, ng, w_qkv, bd, gains)
    return pl.pallas_call(
        _sample_proj_body,
        grid=(1,),
        in_specs=[_whole(a.shape) for a in args],
        out_specs=[_whole(o.shape) for o in out_shape],
        out_shape=out_shape,
        scratch_shapes=[pltpu.VMEM((dec, A_WIDTH), F32)],
        compiler_params=_params(dimension_semantics=("arbitrary",)),
        name="sample_proj",
    )(*args)


def _cache_stream_body(c_ref, q_ref, kn_ref, vn_ref, kvt_ref, bias_ref, *rest, nb, n0, tiles, tile0, heads, head0):
    out_ref, o_ref, ml_ref = rest[-3:]
    window = c_ref.shape[-1]
    width = heads * HEAD_DIM
    reps = heads // tiles
    step = pl.program_id(0)
    head_lane = lax.broadcasted_iota(jnp.int32, (heads, width), 1) // HEAD_DIM
    head_row = lax.broadcasted_iota(jnp.int32, (heads, width), 0)
    own = head_lane == head_row
    spos = lax.broadcasted_iota(jnp.int32, (heads, LANES), 0) + head0
    slane = lax.broadcasted_iota(jnp.int32, (heads, LANES), 1)
    stat = slane == jnp.where(spos % 2 == 1, spos, spos + HEAD_DIM)
    used = jnp.sum(stat.astype(F32), axis=0, keepdims=True) > 0.0
    lane_w = lax.broadcasted_iota(jnp.int32, (HEAD_DIM, window), 1)

    def stacked(tiles_):
        return jnp.concatenate([tiles_[pos % tiles] for pos in range(heads)], axis=0)

    def tiled(row):
        return row if reps == 1 else jnp.concatenate([row] * reps, axis=1)

    for bi in range(nb):
        local = step * nb + bi
        n = n0 + local
        q_row = q_ref[n]
        kn_row = tiled(kn_ref[n])
        vn_row = tiled(vn_ref[n])
        q_bd = jnp.where(own, q_row, 0.0)
        k_all = stacked([c_ref[bi, 0, g].astype(BF16) for g in range(tiles)])
        v_all = stacked([c_ref[bi, 1, g].astype(BF16) for g in range(tiles)])
        s = jnp.dot(q_bd.astype(BF16), k_all, preferred_element_type=F32) + bias_ref[...]
        s_new = jnp.sum(q_bd * kn_row, axis=1, keepdims=True)
        m = jnp.maximum(jnp.max(s, axis=1, keepdims=True), s_new)
        p = jnp.exp2(s - m)
        p_new = jnp.exp2(s_new - m)
        l = jnp.sum(p, axis=1, keepdims=True) + p_new
        o_t = lax.dot_general(p.astype(BF16), v_all, (((1,), (1,)), ((), ())), preferred_element_type=F32)
        o_ref[local] = jnp.sum(jnp.where(own, o_t + p_new * vn_row, 0.0), axis=0, keepdims=True)
        ml_ref[local, :, :LANES] = jnp.sum(jnp.where(stat, m, 0.0), axis=0, keepdims=True)
        ml_ref[local, :, LANES:] = jnp.where(
            used, jnp.sum(jnp.where(stat, l, 0.0), axis=0, keepdims=True), 1.0)
        for c in range(2):
            for g in range(tiles):
                r0 = (tile0 + g) * HEAD_DIM
                new = pltpu.roll(kvt_ref[c, r0:r0 + HEAD_DIM, :], LANES - 1 - n, axis=1)
                if window > LANES:
                    new = jnp.concatenate([new] * (window // LANES), axis=1)
                shifted = pltpu.roll(c_ref[bi, c, g], window - 1, axis=1)
                out_ref[bi, c, g] = jnp.where(lane_w == window - 1, new, shifted)


def _cache_stream_job(cache_t, q, mixer, kv_new, kv_new_t, bias_rows, nb, *, batch=None, tiles=None,
                      prev=None):
    dec, _, kv_heads, _, window = cache_t.shape
    n0, count = batch or (0, dec)
    tile0, n_tiles = tiles or (0, kv_heads)
    assert tiles is None or kv_heads == N_HEADS
    heads = n_tiles * (N_HEADS // kv_heads)
    head0 = tile0 * (N_HEADS // kv_heads)
    nb = min(nb, count)
    blk = (nb, 2, n_tiles, HEAD_DIM, window)
    here = lambda i: (n0 // nb + i, 0, tile0 // n_tiles, 0, 0)
    kv_width = n_tiles * HEAD_DIM
    q_width = heads * HEAD_DIM
    in_specs = [pl.BlockSpec(blk, here),
                pl.BlockSpec((dec, 1, q_width), lambda i: (0, 0, (mixer * N_HEADS + head0) // heads)),
                pl.BlockSpec((None, dec, 1, kv_width), lambda i: (0, 0, 0, tile0 // n_tiles)),
                pl.BlockSpec((None, dec, 1, kv_width), lambda i: (1, 0, 0, tile0 // n_tiles)),
                _resident(kv_new_t.shape), _resident((heads, window))]
    args = [cache_t, q, kv_new, kv_new, kv_new_t, bias_rows[head0:head0 + heads]]
    aliases = {}
    if prev is not None:
        in_specs.append(pl.BlockSpec(memory_space=pl.ANY))
        args.append(prev)
        aliases = {len(args) - 1: 0}
    return _Job(
        body=functools.partial(_cache_stream_body, nb=nb, n0=n0, tiles=n_tiles, tile0=tile0, heads=heads,
                               head0=head0),
        steps=count // nb,
        in_specs=in_specs,
        out_specs=[pl.BlockSpec(blk, here),
                   pl.BlockSpec((count, 1, q_width), lambda i: (0, 0, 0)),
                   pl.BlockSpec((count, 1, 2 * LANES), lambda i: (0, 0, 0))],
        out_shape=[jax.ShapeDtypeStruct(cache_t.shape, F32),
                   jax.ShapeDtypeStruct((count, 1, q_width), F32),
                   jax.ShapeDtypeStruct((count, 1, 2 * LANES), F32)],
        scratch_shapes=[],
        args=args,
        name="cache_stream",
        aliases=aliases)


def _sample_merge_body(x_ref, ng_ref, wg_ref, pa_ref, pb_ref, wo_ref, e_ref, sink_ref,
                       o1_ref, l1_ref, o2_ref, l2_ref, o3_ref, l3_ref, ob_ref, lb_ref, y_ref):
    y_ref[...] = _merge_math(x_ref[...], ng_ref, wg_ref, pa_ref, pb_ref, wo_ref, e_ref, sink_ref,
                             [o1_ref[...], o2_ref[...], o3_ref[...]],
                             [l1_ref[...], l2_ref[...], l3_ref[...]], ob_ref[...], lb_ref[...])


def _sample_merge(x, ng, w_gates, p_a, p_b, w_o, expand, sink_row, o_list, ml_list):
    args = [x, ng, w_gates, p_a, p_b, w_o, expand, sink_row]
    for o, l in zip(o_list, ml_list):
        args += [o, l]
    return pl.pallas_call(
        _sample_merge_body,
        grid=(1,),
        in_specs=[_whole(a.shape) for a in args],
        out_specs=_whole(x.shape),
        out_shape=jax.ShapeDtypeStruct(x.shape, F32),
        compiler_params=_params(dimension_semantics=("arbitrary",)),
        name="sample_merge",
    )(*args)


def _buffer_bias_rows(window, dilation):
    r = np.arange(window)
    steps = (window - r) // dilation
    slopes = np.exp2(-np.arange(1, N_HEADS + 1, dtype=np.float32))
    bias = -slopes[:, None] * steps[None].astype(np.float32)
    return np.where((r % dilation == 0)[None], bias * LOG2E, np.float32(NEG)).astype(np.float32)


def _band_bias_table():
    q = np.arange(BLOCK)[:, None]
    kk = np.arange(2 * BLOCK)[None, :]
    delta = q + BLOCK - kk
    valid = (delta >= 0) & (delta <= BLOCK)
    slopes = np.exp2(-np.arange(1, N_HEADS + 1, dtype=np.float32))
    bias = -slopes[:, None, None] * delta[None].astype(np.float32)
    normal = np.where(valid[None], bias, np.float32(NEG))
    first = np.where((valid & (kk >= BLOCK))[None], bias, np.float32(NEG))
    return (np.stack([first, normal]) * LOG2E).astype(np.float32)


def _block_diag_ones():
    i = np.arange(MXU_WIDTH)
    return (i[:, None] // HEAD_DIM == i[None, :] // HEAD_DIM).astype(np.float32)


def _head_expand_matrix():
    e = np.zeros((LANES, A_WIDTH), np.float32)
    for pos in range(N_HEADS):
        e[_stat_lane(pos), pos * HEAD_DIM:(pos + 1) * HEAD_DIM] = 1.0
    return e


def _b_column_order():
    return np.concatenate([np.arange(HEAD_DIM) + HEAD_DIM * hd for hd in B_HEAD_ORDER])


def _prepare_weights(norm_gain, w_in, qk_norm_a, qk_norm_b, b_sinks, w_branch_a, w_branch_b, w_out):
    w = w_in[0]
    order = _b_column_order()
    b_q0 = 9 * A_WIDTH
    w_qkv = jnp.concatenate([w[:, :b_q0], w[:, b_q0:b_q0 + A_WIDTH][:, order],
                             w[:, b_q0 + A_WIDTH:QKV_WIDTH]], axis=1).astype(BF16)
    g0 = QKV_WIDTH
    w_gates = jnp.concatenate([w[:, g0:g0 + A_WIDTH], w[:, g0 + A_WIDTH:g0 + 2 * A_WIDTH][:, order],
                               w[:, g0 + 2 * A_WIDTH:]], axis=1).astype(BF16)
    scale = HEAD_DIM ** -0.5 * LOG2E
    rows = []
    for g in range(3):
        rows.append(jnp.tile(qk_norm_a[0, g, 0], N_HEADS) * scale)
        rows.append(jnp.tile(qk_norm_a[0, g, 1], N_HEADS))
    rows.append(jnp.tile(qk_norm_b[0, 0], N_HEADS) * scale)
    rows.append(jnp.tile(qk_norm_b[0, 1], N_HEADS))
    gains = jnp.stack(rows).astype(F32)
    lanes = np.array([_stat_lane(pos) for pos in range(N_HEADS)])
    sink_row = jnp.zeros((1, LANES), F32).at[0, lanes].set(b_sinks[0][np.array(B_HEAD_ORDER)].astype(F32) * LOG2E)
    return dict(ng=norm_gain.astype(F32), w_qkv=w_qkv, w_gates=w_gates, gains=gains, sink_row=sink_row,
                p_a=w_branch_a[0].astype(BF16), p_b=w_branch_b[0][order].astype(BF16),
                w_o=w_out[0].astype(BF16))


def _cache_out(t):
    nb, two, width, rows = t.shape
    t = t.reshape(nb, two, width // HEAD_DIM, HEAD_DIM, rows)
    return jnp.transpose(t, (0, 4, 1, 2, 3))[None]


STREAM_BATCH = (2, 1, 1, 2)


def _layer(x_prompt, x_sample, caches, wts, consts):
    nb, seq, _ = x_prompt.shape
    dec = x_sample.shape[0]
    x2 = x_sample.reshape(dec, D_MODEL)
    q_s, kv_a, kv_b, kvt_a, kvt_b = _sample_proj(x2, wts["ng"], wts["w_qkv"], consts["bd"], wts["gains"])

    def stream_job(mixer, **part):
        cache_t = jnp.transpose(caches[mixer][0], (0, 2, 3, 4, 1))
        kv_new, kv_new_t = (kv_b, kvt_b) if mixer == 3 else (kv_a[mixer], kvt_a[mixer])
        return _cache_stream_job(cache_t, q_s[:, None, :], mixer, kv_new[:, :, None, :], kv_new_t,
                                 consts["rows"][mixer], STREAM_BATCH[mixer], **part)

    def attention_job(g):
        d = A_DILATIONS[g]
        shape = (nb * d, seq // d, A_WIDTH)
        return _attention_job(*(a.reshape(shape) for a in qkv[3 * g:3 * g + 3]), consts["bias"], NATURAL_ORDER)

    half_heads, half_batch = N_HEADS // 2, dec // 2
    qkv, a3_lo = _run_pair(
        _prompt_proj_job(x_prompt, wts["ng"], wts["w_qkv"], consts["bd"], wts["gains"]),
        stream_job(2, tiles=(0, half_heads)))
    prompt_caches = [_cache_out(c) for c in
                     _prompt_cache(x_prompt, wts["ng"], wts["w_qkv"], consts["bd"], wts["gains"])]
    attn = [None] * 3
    attn[0], a1 = _run_pair(attention_job(0), stream_job(0))
    attn[1], a2_lo = _run_pair(attention_job(1), stream_job(1, batch=(0, half_batch)))
    attn[2], a2_hi = _run_pair(attention_job(2), stream_job(1, batch=(half_batch, half_batch), prev=a2_lo[0]))
    (o_b, ml_b), b = _run_pair(
        _attention_job(qkv[9], qkv[10], qkv[11], consts["bias"], B_HEAD_ORDER), stream_job(3))
    o_list, ml_list = [a[0] for a in attn], [a[1] for a in attn]
    (y_prompt,), a3_hi = _run_pair(
        _prompt_merge_job(x_prompt, wts["ng"], wts["w_gates"], wts["p_a"], wts["p_b"], wts["w_o"],
                          consts["expand"], wts["sink_row"], o_list, ml_list, o_b, ml_b),
        stream_job(2, tiles=(half_heads, half_heads), prev=a3_lo[0]))

    rows = lambda a: a.reshape(a.shape[0], a.shape[2])
    ml3_lo, ml3_hi = rows(a3_lo[2]), rows(a3_hi[2])
    ml3 = jnp.concatenate([ml3_lo[:, :LANES] + ml3_hi[:, :LANES],
                           ml3_lo[:, LANES:] + ml3_hi[:, LANES:] - 1.0], axis=1)
    sample_o = [rows(a1[1]), jnp.concatenate([rows(a2_lo[1]), rows(a2_hi[1])], axis=0),
                jnp.concatenate([rows(a3_lo[1]), rows(a3_hi[1])], axis=1), rows(b[1])]
    sample_ml = [rows(a1[2]), jnp.concatenate([rows(a2_lo[2]), rows(a2_hi[2])], axis=0), ml3, rows(b[2])]
    sample_caches = [jnp.transpose(t, (0, 4, 1, 2, 3))[None] for t in (a1[0], a2_hi[0], a3_hi[0], b[0])]
    y_sample = _sample_merge(x2, wts["ng"], wts["w_gates"], wts["p_a"], wts["p_b"], wts["w_o"],
                             consts["expand"], wts["sink_row"], sample_o, sample_ml)
    return y_prompt, y_sample.reshape(dec, 1, D_MODEL), prompt_caches, sample_caches


def _constants():
    assert all(B_HEAD_ORDER[pos] // (N_HEADS // B_KV_HEADS) == pos % B_KV_HEADS for pos in range(N_HEADS))
    rows = [jnp.asarray(_buffer_bias_rows(w, d)) for w, d in zip(A_WINDOWS, A_DILATIONS)]
    rows.append(jnp.asarray(_buffer_bias_rows(B_WINDOW, 1)[np.array(B_HEAD_ORDER)]))
    return dict(bias=jnp.asarray(_band_bias_table()), bd=jnp.asarray(_block_diag_ones(), BF16),
                expand=jnp.asarray(_head_expand_matrix(), BF16), rows=rows)


def kernel(x_prompt, x_sample, cache_a1_kv, cache_a2_kv, cache_a3_kv, cache_b_kv, norm_gain, w_in,
           qk_norm_a, qk_norm_b, b_sinks, w_branch_a, w_branch_b, w_out):
    wts = _prepare_weights(norm_gain, w_in, qk_norm_a, qk_norm_b, b_sinks, w_branch_a, w_branch_b, w_out)
    consts = _constants()
    y_prompt, y_sample, pc, sc = _layer(x_prompt, x_sample, (cache_a1_kv, cache_a2_kv, cache_a3_kv, cache_b_kv),
                                        wts, consts)
    return (y_prompt, y_sample, pc[0], pc[1], pc[2], pc[3], sc[0], sc[1], sc[2], sc[3])
```

```python
import functools
from typing import Callable, NamedTuple, Optional, Sequence

import numpy as np
import jax
import jax.numpy as jnp
from jax import lax
from jax.experimental import pallas as pl
from jax.experimental.pallas import tpu as pltpu

F32 = jnp.float32
BF16 = jnp.bfloat16

D_MODEL = 1024
HEAD_DIM = 64
N_HEADS = 8
A_WINDOWS = (128, 512, 2048)
A_DILATIONS = (1, 4, 16)
B_WINDOW = 128
B_KV_HEADS = 2
A_WIDTH = N_HEADS * HEAD_DIM
B_KV_WIDTH = B_KV_HEADS * HEAD_DIM
QKV_WIDTH = 9 * A_WIDTH + A_WIDTH + 2 * B_KV_WIDTH
BLOCK = 128
Q_ROWS = 512
NORM_EPS = 1e-6
NEG = -1e30
LOG2E = float(np.log2(np.e))
LANES = 128
MXU_WIDTH = 256
TOKEN_TILE = 256
CACHE_TILE = 512
ROW_TILE = 256
VMEM_LIMIT = 56 * 1024 * 1024
B_HEAD_ORDER = (0, 4, 1, 5, 2, 6, 3, 7)
NATURAL_ORDER = tuple(range(N_HEADS))

_CHUNKS = []
for _g in range(3):
    _CHUNKS += [(1536 * _g, 512, 2 * _g), (1536 * _g + 512, 512, 2 * _g + 1), (1536 * _g + 1024, 512, None)]
_CHUNKS += [(4608, 512, 6), (5120, 128, 7), (5248, 128, None)]


def _params(**kw):
    return pltpu.CompilerParams(vmem_limit_bytes=VMEM_LIMIT, **kw)


def _resident(shape):
    nd = len(shape)
    return pl.BlockSpec(shape, lambda *_: (0,) * nd, pipeline_mode=pl.Buffered(1))


def _whole(shape):
    nd = len(shape)
    return pl.BlockSpec(shape, lambda *_: (0,) * nd)


class _Job(NamedTuple):
    body: Callable
    steps: int
    in_specs: Sequence
    out_specs: Sequence
    out_shape: Sequence
    scratch_shapes: Sequence
    args: Sequence
    name: str
    aliases: Optional[dict] = None
    phases: int = 1


def _run(job):
    def body(*refs):
        for _ in job.body(*refs):
            pass

    return pl.pallas_call(
        body, grid=(job.steps,), in_specs=list(job.in_specs), out_specs=list(job.out_specs),
        out_shape=list(job.out_shape), scratch_shapes=list(job.scratch_shapes),
        input_output_aliases=dict(job.aliases or {}),
        compiler_params=_params(dimension_semantics=("arbitrary",)), name=job.name)(*job.args)


def _run_pair(a, b, interleave):
    if a.steps != b.steps:
        return _run(a), _run(b)
    n_in = (len(a.in_specs), len(b.in_specs))
    n_out = (len(a.out_specs), len(b.out_specs))
    n_scr = len(a.scratch_shapes)

    def body(*refs):
        ins, rest = refs[:sum(n_in)], refs[sum(n_in):]
        outs, scr = rest[:sum(n_out)], rest[sum(n_out):]
        gens = [a.body(*ins[:n_in[0]], *outs[:n_out[0]], *scr[:n_scr]),
                b.body(*ins[n_in[0]:], *outs[n_out[0]:], *scr[n_scr:])]
        credit = [0, 0]
        live = [True, True]
        while any(live):
            ahead = credit[0] * b.phases > credit[1] * a.phases if interleave else False
            k = 0 if (live[0] and not (live[1] and ahead)) else 1
            try:
                next(gens[k])
                credit[k] += 1
            except StopIteration:
                live[k] = False

    aliases = dict(a.aliases or {})
    aliases.update({n_in[0] + i: n_out[0] + o for i, o in (b.aliases or {}).items()})
    outs = pl.pallas_call(
        body, grid=(a.steps,), in_specs=list(a.in_specs) + list(b.in_specs),
        out_specs=list(a.out_specs) + list(b.out_specs), out_shape=list(a.out_shape) + list(b.out_shape),
        scratch_shapes=list(a.scratch_shapes) + list(b.scratch_shapes), input_output_aliases=aliases,
        compiler_params=_params(dimension_semantics=("arbitrary",)), name=a.name + "_" + b.name,
    )(*a.args, *b.args)
    return outs[:n_out[0]], outs[n_out[0]:]


def _rms_rows(x, gain_row):
    ms = jnp.mean(x * x, axis=-1, keepdims=True)
    return (x * lax.rsqrt(ms + NORM_EPS)) * gain_row


def _proj_chunk(h, w_ref, bd_ref, gain_ref, chunk):
    off, wd, gain_row = chunk
    parts = []
    for r0 in range(0, h.shape[0], ROW_TILE):
        acc = jnp.dot(h[r0:r0 + ROW_TILE], w_ref[:, off:off + wd], preferred_element_type=F32)
        if gain_row is not None:
            sq = (acc * acc).astype(BF16)
            step = min(wd, MXU_WIDTH)
            ss = jnp.concatenate(
                [jnp.dot(sq[:, c0:c0 + step], bd_ref[:step, :step], preferred_element_type=F32)
                 for c0 in range(0, wd, step)], axis=1)
            acc = (acc * lax.rsqrt(ss * (1.0 / HEAD_DIM) + NORM_EPS)) * gain_ref[gain_row:gain_row + 1, :wd]
        parts.append(acc)
    return parts[0] if len(parts) == 1 else jnp.concatenate(parts, axis=0)


def _sigmoid(z):
    return 1.0 / (1.0 + jnp.exp(-z))


def _prompt_proj_body(x_ref, ng_ref, w_ref, bd_ref, gain_ref, *refs, tt):
    outs = refs[:12]
    perm_ref = refs[12]
    h = _rms_rows(x_ref[0], ng_ref[...]).astype(BF16)
    for ci, chunk in enumerate(_CHUNKS):
        y = _proj_chunk(h, w_ref, bd_ref, gain_ref, chunk)
        mixer, role = divmod(ci, 3)
        o_ref = outs[ci]
        d = A_DILATIONS[mixer] if mixer < 3 else 1
        if mixer == 3:
            o_ref[0] = y.astype(BF16)
        elif d == 1:
            o_ref[0, 0] = y.astype(BF16)
        else:
            rows = tt // d
            for j in range(A_WIDTH // LANES):
                perm_ref[j] = y[:, j * LANES:(j + 1) * LANES]
            for r in range(d):
                for j in range(A_WIDTH // LANES):
                    o_ref[0, r, :, j * LANES:(j + 1) * LANES] = (
                        perm_ref[j, pl.ds(r, rows, stride=d), :].astype(BF16))
        yield


def _prompt_proj_job(x, ng, w_qkv, bd, gains):
    nb, seq, _ = x.shape
    tt = TOKEN_TILE
    tiles = seq // tt
    out_shape, out_specs = [], []
    for g in range(3):
        d = A_DILATIONS[g]
        for _ in range(3):
            out_shape.append(jax.ShapeDtypeStruct((nb, d, seq // d, A_WIDTH), BF16))
            out_specs.append(pl.BlockSpec((1, d, tt // d, A_WIDTH), lambda i: (i // tiles, 0, i % tiles, 0)))
    for wd in (A_WIDTH, B_KV_WIDTH, B_KV_WIDTH):
        out_shape.append(jax.ShapeDtypeStruct((nb, seq, wd), BF16))
        out_specs.append(pl.BlockSpec((1, tt, wd), lambda i: (i // tiles, i % tiles, 0)))
    return _Job(
        body=functools.partial(_prompt_proj_body, tt=tt),
        steps=nb * tiles,
        in_specs=[pl.BlockSpec((1, tt, D_MODEL), lambda i: (i // tiles, i % tiles, 0)),
                  _resident(ng.shape), _resident(w_qkv.shape), _resident(bd.shape), _resident(gains.shape)],
        out_specs=out_specs,
        out_shape=out_shape,
        scratch_shapes=[pltpu.VMEM((A_WIDTH // LANES, tt, LANES), F32)],
        args=(x, ng, w_qkv, bd, gains),
        name="prompt_proj",
        phases=len(_CHUNKS))


def _prompt_cache_body(x_ref, ng_ref, w_ref, bd_ref, gain_ref, c1_ref, c2_ref, c3_ref, cb_ref, y_scr, *, tt):
    j = pl.program_id(1)
    last = pl.num_programs(1) - 1
    h = _rms_rows(x_ref[0], ng_ref[...]).astype(BF16)

    def kv_t(rows, mixer, c):
        wd = _CHUNKS[3 * mixer + 1 + c][1]
        y_scr[:rows.shape[0], :wd] = _proj_chunk(rows, w_ref, bd_ref, gain_ref, _CHUNKS[3 * mixer + 1 + c])
        return y_scr[:rows.shape[0], :wd].T

    for c in range(2):
        c3_ref[0, c] = kv_t(h, 2, c)

    @pl.when(j == last)
    def _():
        tail = h[tt - BLOCK:]
        for c in range(2):
            c2_ref[0, c] = kv_t(h, 1, c)
            c1_ref[0, c] = kv_t(tail, 0, c)
            cb_ref[0, c] = kv_t(tail, 3, c)


def _prompt_cache(x, ng, w_qkv, bd, gains):
    nb, seq, _ = x.shape
    tt = CACHE_TILE
    assert A_WINDOWS == (BLOCK, tt, 4 * tt) and B_WINDOW == BLOCK
    n_tiles = A_WINDOWS[2] // tt
    first = seq // tt - n_tiles
    out_shape = [jax.ShapeDtypeStruct((nb, 2, A_WIDTH, w), F32) for w in A_WINDOWS]
    out_shape.append(jax.ShapeDtypeStruct((nb, 2, B_KV_WIDTH, B_WINDOW), F32))
    out_specs = [pl.BlockSpec((1, 2, A_WIDTH, BLOCK), lambda n, j: (n, 0, 0, 0)),
                 pl.BlockSpec((1, 2, A_WIDTH, tt), lambda n, j: (n, 0, 0, 0)),
                 pl.BlockSpec((1, 2, A_WIDTH, tt), lambda n, j: (n, 0, 0, j)),
                 pl.BlockSpec((1, 2, B_KV_WIDTH, BLOCK), lambda n, j: (n, 0, 0, 0))]
    return pl.pallas_call(
        functools.partial(_prompt_cache_body, tt=tt),
        grid=(nb, n_tiles),
        in_specs=[pl.BlockSpec((1, tt, D_MODEL), lambda n, j: (n, first + j, 0)),
                  _resident(ng.shape), _resident(w_qkv.shape), _resident(bd.shape), _resident(gains.shape)],
        out_specs=out_specs,
        out_shape=out_shape,
        scratch_shapes=[pltpu.VMEM((tt, A_WIDTH), F32)],
        compiler_params=_params(dimension_semantics=("arbitrary", "arbitrary")),
        name="prompt_cache",
    )(x, ng, w_qkv, bd, gains)


def _stat_lane(pos):
    return pos if pos % 2 else HEAD_DIM + pos


def _softmax_block(s, bias, v_ext):
    s = s + bias
    m = jnp.max(s, axis=-1, keepdims=True)
    p = jnp.exp2(s - m).astype(BF16)
    return m, jnp.dot(p, v_ext, preferred_element_type=F32)


def _attn_body(q_ref, kp_ref, kc_ref, vp_ref, vc_ref, bias_ref, o_ref, ml_ref,
               *, shared_kv, head_of_pos, per_seq):
    n_blocks = Q_ROWS // BLOCK
    first = jnp.minimum(pl.program_id(0) % per_seq, 1)
    lane = lax.broadcasted_iota(jnp.int32, (BLOCK, LANES), 1)
    lo = lane < HEAD_DIM
    lo_kv = lax.broadcasted_iota(jnp.int32, (BLOCK + Q_ROWS, LANES), 1) < HEAD_DIM
    one = jnp.ones((), BF16)
    ml_ref[0, :, :LANES] = jnp.zeros((Q_ROWS, LANES), F32)
    ml_ref[0, :, LANES:] = jnp.ones((Q_ROWS, LANES), F32)

    def load_kv(cols):
        k_all = jnp.concatenate([kp_ref[0, :, cols], kc_ref[0, :, cols]], axis=0)
        v_all = jnp.concatenate([vp_ref[0, :, cols], vc_ref[0, :, cols]], axis=0)
        return k_all, (jnp.where(lo_kv, v_all, one), jnp.where(lo_kv, one, v_all))

    if shared_kv:
        k_all, v_ext = load_kv(slice(0, LANES))
    for i in range(N_HEADS // 2):
        cols = slice(i * LANES, (i + 1) * LANES)
        if not shared_kv:
            k_all, v_ext = load_kv(cols)
        for b in range(n_blocks):
            sel = first if b == 0 else 1
            keys = slice(b * BLOCK, (b + 2) * BLOCK)
            rows = slice(b * BLOCK, (b + 1) * BLOCK)
            qp = q_ref[0, rows, cols]
            zero = jnp.zeros_like(qp)
            q2 = jnp.concatenate([jnp.where(lo, qp, zero), jnp.where(lo, zero, qp)], axis=0)
            s2 = lax.dot_general(q2, k_all[keys], (((1,), (1,)), ((), ())), preferred_element_type=F32)
            m_pair, pv_pair = [], []
            for hh in range(2):
                m, pv = _softmax_block(s2[hh * BLOCK:(hh + 1) * BLOCK],
                                       bias_ref[sel, head_of_pos[2 * i + hh]], v_ext[hh][keys])
                m_pair.append(m)
                pv_pair.append(pv)
            o_ref[0, rows, cols] = jnp.where(lo, pv_pair[0], pv_pair[1])
            for hh in range(2):
                at = _stat_lane(2 * i + hh)
                ml_ref[0, rows, at:at + 1] = m_pair[hh]
                ml_ref[0, rows, LANES + at:LANES + at + 1] = pv_pair[hh][:, at:at + 1]
            yield


def _attention_job(q, k, v, bias, head_of_pos):
    ns, steps, _ = q.shape
    kv_width = k.shape[-1]
    ratio = Q_ROWS // BLOCK
    per_seq = steps // Q_ROWS
    here = lambda i: (i // per_seq, i % per_seq, 0)
    qspec = pl.BlockSpec((1, Q_ROWS, A_WIDTH), here)
    cur = pl.BlockSpec((1, Q_ROWS, kv_width), here)
    prev = pl.BlockSpec((1, BLOCK, kv_width),
                        lambda i: (i // per_seq, jnp.maximum((i % per_seq) * ratio - 1, 0), 0))
    return _Job(
        body=functools.partial(_attn_body, shared_kv=kv_width == LANES, head_of_pos=head_of_pos,
                               per_seq=per_seq),
        steps=ns * per_seq,
        in_specs=[qspec, prev, cur, prev, cur, _resident(bias.shape)],
        out_specs=[qspec, pl.BlockSpec((1, Q_ROWS, 2 * LANES), here)],
        out_shape=[jax.ShapeDtypeStruct((ns, steps, A_WIDTH), F32),
                   jax.ShapeDtypeStruct((ns, steps, 2 * LANES), F32)],
        scratch_shapes=[],
        args=(q, k, k, v, v, bias),
        name="attention",
        phases=(N_HEADS // 2) * (Q_ROWS // BLOCK))


def _head_expand(w, e_ref):
    return jnp.dot(w.astype(BF16), e_ref[...], preferred_element_type=F32)


MERGE_PHASES = 8


def _merge_math(store, x, ng_ref, wg_ref, pa_ref, pb_ref, wo_ref, e_ref, sink_ref, o_groups, ml_groups, o_b,
                ml_b):
    ms = [ml[:, :LANES] for ml in ml_groups]
    ls = [ml[:, LANES:] for ml in ml_groups]
    big = jnp.maximum(jnp.maximum(ms[0], ms[1]), ms[2])
    es = [jnp.exp2(m - big) for m in ms]
    inv = 1.0 / (es[0] * ls[0] + es[1] * ls[1] + es[2] * ls[2])
    o_a = None
    for e, o in zip(es, o_groups):
        term = _head_expand(e * inv, e_ref) * o
        o_a = term if o_a is None else o_a + term
    inv_b = 1.0 / (ml_b[:, LANES:] + jnp.exp2(sink_ref[...] - ml_b[:, :LANES]))
    o_b = _head_expand(inv_b, e_ref) * o_b
    yield
    h = _rms_rows(x, ng_ref[...]).astype(BF16)
    gate = []
    for c0, c1 in ((0, A_WIDTH), (A_WIDTH, 2 * A_WIDTH), (2 * A_WIDTH, 2 * A_WIDTH + D_MODEL),
                   (2 * A_WIDTH + D_MODEL, 2 * A_WIDTH + 2 * D_MODEL)):
        gate.append(jnp.dot(h, wg_ref[:, c0:c1], preferred_element_type=F32))
        yield
    z_a, z_b, g_a, g_b = gate
    a = (o_a * (z_a * _sigmoid(z_a))).astype(BF16)
    b = (o_b * (z_b * _sigmoid(z_b))).astype(BF16)
    mixed = _sigmoid(g_a) * jnp.dot(a, pa_ref[...], preferred_element_type=F32)
    yield
    mixed = mixed + _sigmoid(g_b) * jnp.dot(b, pb_ref[...], preferred_element_type=F32)
    yield
    store(x + jnp.dot(mixed.astype(BF16), wo_ref[...], preferred_element_type=F32))
    yield


def _prompt_merge_body(x_ref, ng_ref, wg_ref, pa_ref, pb_ref, wo_ref, e_ref, sink_ref,
                       o1_ref, l1_ref, o2_ref, l2_ref, o3_ref, l3_ref, ob_ref, lb_ref, y_ref,
                       o_scr, l_scr, *, tt):
    for gi, (o_ref, l_ref, d) in enumerate(((o2_ref, l2_ref, 4), (o3_ref, l3_ref, 16))):
        rows = tt // d
        for r in range(d):
            for j in range(2):
                l_scr[gi, j, pl.ds(r, rows, stride=d), :] = l_ref[0, r, :, j * LANES:(j + 1) * LANES]
            for j in range(A_WIDTH // LANES):
                o_scr[gi, j, pl.ds(r, rows, stride=d), :] = o_ref[0, r, :, j * LANES:(j + 1) * LANES]
        yield
    o_groups = [o1_ref[0, 0]] + [
        jnp.concatenate([o_scr[gi, j] for j in range(A_WIDTH // LANES)], axis=1) for gi in range(2)]
    ml_groups = [l1_ref[0, 0]] + [jnp.concatenate([l_scr[gi, 0], l_scr[gi, 1]], axis=1) for gi in range(2)]

    def store(y):
        y_ref[0] = y

    yield from _merge_math(store, x_ref[0], ng_ref, wg_ref, pa_ref, pb_ref, wo_ref, e_ref, sink_ref,
                           o_groups, ml_groups, ob_ref[0], lb_ref[0])


def _prompt_merge_job(x, ng, w_gates, p_a, p_b, w_o, expand, sink_row, o_list, ml_list, o_b, ml_b):
    nb, seq, _ = x.shape
    tt = TOKEN_TILE
    tiles = seq // tt
    tile3 = lambda i: (i // tiles, i % tiles, 0)
    tile4 = lambda i: (i // tiles, 0, i % tiles, 0)
    in_specs = [pl.BlockSpec((1, tt, D_MODEL), tile3)]
    in_specs += [_resident(a.shape) for a in (ng, w_gates, p_a, p_b, w_o, expand, sink_row)]
    args = [x, ng, w_gates, p_a, p_b, w_o, expand, sink_row]
    for g in range(3):
        d = A_DILATIONS[g]
        in_specs.append(pl.BlockSpec((1, d, tt // d, A_WIDTH), tile4))
        in_specs.append(pl.BlockSpec((1, d, tt // d, 2 * LANES), tile4))
        args += [o_list[g].reshape(nb, d, seq // d, A_WIDTH), ml_list[g].reshape(nb, d, seq // d, 2 * LANES)]
    in_specs.append(pl.BlockSpec((1, tt, A_WIDTH), tile3))
    in_specs.append(pl.BlockSpec((1, tt, 2 * LANES), tile3))
    args += [o_b, ml_b]
    return _Job(
        body=functools.partial(_prompt_merge_body, tt=tt),
        steps=nb * tiles,
        in_specs=in_specs,
        out_specs=[pl.BlockSpec((1, tt, D_MODEL), tile3)],
        out_shape=[jax.ShapeDtypeStruct((nb, seq, D_MODEL), F32)],
        scratch_shapes=[pltpu.VMEM((2, A_WIDTH // LANES, tt, LANES), F32),
                        pltpu.VMEM((2, 2, tt, LANES), F32)],
        args=args,
        name="prompt_merge",
        phases=MERGE_PHASES + 2)


def _sample_proj_body(x_ref, ng_ref, w_ref, bd_ref, gain_ref, q_ref, kva_ref, kvb_ref, kvta_ref, kvtb_ref,
                      y_scr):
    dec = x_ref.shape[0]
    h = _rms_rows(x_ref[...], ng_ref[...]).astype(BF16)
    kvta_ref[...] = jnp.zeros_like(kvta_ref)
    kvtb_ref[...] = jnp.zeros_like(kvtb_ref)
    for ci, chunk in enumerate(_CHUNKS):
        wd = chunk[1]
        y = _proj_chunk(h, w_ref, bd_ref, gain_ref, chunk)
        mixer, role = divmod(ci, 3)
        if role == 0:
            q_ref[:, mixer * A_WIDTH:(mixer + 1) * A_WIDTH] = y
            continue
        y_scr[:, :wd] = y
        y_t = y_scr[:, :wd].T
        if mixer < 3:
            kva_ref[mixer, role - 1] = y
            kvta_ref[mixer, role - 1, :, :dec] = y_t
        else:
            kvb_ref[role - 1] = y
            kvtb_ref[role - 1, :, :dec] = y_t


def _sample_proj(x, ng, w_qkv, bd, gains):
    dec = x.shape[0]
    assert dec <= LANES
    out_shape = [jax.ShapeDtypeStruct((dec, 4 * A_WIDTH), F32),
                 jax.ShapeDtypeStruct((3, 2, dec, A_WIDTH), F32),
                 jax.ShapeDtypeStruct((2, dec, B_KV_WIDTH), F32),
                 jax.ShapeDtypeStruct((3, 2, A_WIDTH, LANES), F32),
                 jax.ShapeDtypeStruct((2, B_KV_WIDTH, LANES), F32)]
    args = (x, ng, w_qkv, bd, gains)
    return pl.pallas_call(
        _sample_proj_body,
        grid=(1,),
        in_specs=[_whole(a.shape) for a in args],
        out_specs=[_whole(o.shape) for o in out_shape],
        out_shape=out_shape,
        scratch_shapes=[pltpu.VMEM((dec, A_WIDTH), F32)],
        compiler_params=_params(dimension_semantics=("arbitrary",)),
        name="sample_proj",
    )(*args)


def _cache_stream_body(c_ref, q_ref, kn_ref, vn_ref, kvt_ref, bias_ref, *rest, nb, n0, tiles, tile0, heads, head0):
    out_ref, o_ref, ml_ref = rest[-3:]
    window = c_ref.shape[-1]
    width = heads * HEAD_DIM
    reps = heads // tiles
    step = pl.program_id(0)
    head_lane = lax.broadcasted_iota(jnp.int32, (heads, width), 1) // HEAD_DIM
    head_row = lax.broadcasted_iota(jnp.int32, (heads, width), 0)
    own = head_lane == head_row
    spos = lax.broadcasted_iota(jnp.int32, (heads, LANES), 0) + head0
    slane = lax.broadcasted_iota(jnp.int32, (heads, LANES), 1)
    stat = slane == jnp.where(spos % 2 == 1, spos, spos + HEAD_DIM)
    used = jnp.sum(stat.astype(F32), axis=0, keepdims=True) > 0.0
    lane_w = lax.broadcasted_iota(jnp.int32, (HEAD_DIM, window), 1)

    def stacked(tiles_):
        return jnp.concatenate([tiles_[pos % tiles] for pos in range(heads)], axis=0)

    def tiled(row):
        return row if reps == 1 else jnp.concatenate([row] * reps, axis=1)

    for bi in range(nb):
        local = step * nb + bi
        n = n0 + local
        q_row = q_ref[n]
        kn_row = tiled(kn_ref[n])
        vn_row = tiled(vn_ref[n])
        q_bd = jnp.where(own, q_row, 0.0)
        k_all = stacked([c_ref[bi, 0, g].astype(BF16) for g in range(tiles)])
        v_all = stacked([c_ref[bi, 1, g].astype(BF16) for g in range(tiles)])
        s = jnp.dot(q_bd.astype(BF16), k_all, preferred_element_type=F32) + bias_ref[...]
        s_new = jnp.sum(q_bd * kn_row, axis=1, keepdims=True)
        m = jnp.maximum(jnp.max(s, axis=1, keepdims=True), s_new)
        p = jnp.exp2(s - m)
        p_new = jnp.exp2(s_new - m)
        l = jnp.sum(p, axis=1, keepdims=True) + p_new
        o_t = lax.dot_general(p.astype(BF16), v_all, (((1,), (1,)), ((), ())), preferred_element_type=F32)
        o_ref[local] = jnp.sum(jnp.where(own, o_t + p_new * vn_row, 0.0), axis=0, keepdims=True)
        ml_ref[local, :, :LANES] = jnp.sum(jnp.where(stat, m, 0.0), axis=0, keepdims=True)
        ml_ref[local, :, LANES:] = jnp.where(
            used, jnp.sum(jnp.where(stat, l, 0.0), axis=0, keepdims=True), 1.0)
        yield
        for c in range(2):
            for g in range(tiles):
                r0 = (tile0 + g) * HEAD_DIM
                new = pltpu.roll(kvt_ref[c, r0:r0 + HEAD_DIM, :], LANES - 1 - n, axis=1)
                if window > LANES:
                    new = jnp.concatenate([new] * (window // LANES), axis=1)
                shifted = pltpu.roll(c_ref[bi, c, g], window - 1, axis=1)
                out_ref[bi, c, g] = jnp.where(lane_w == window - 1, new, shifted)
                yield


def _cache_stream_job(cache_t, q, mixer, kv_new, kv_new_t, bias_rows, nb, *, batch=None, tiles=None,
                      prev=None):
    dec, _, kv_heads, _, window = cache_t.shape
    n0, count = batch or (0, dec)
    tile0, n_tiles = tiles or (0, kv_heads)
    assert tiles is None or kv_heads == N_HEADS
    heads = n_tiles * (N_HEADS // kv_heads)
    head0 = tile0 * (N_HEADS // kv_heads)
    nb = min(nb, count)
    blk = (nb, 2, n_tiles, HEAD_DIM, window)
    here = lambda i: (n0 // nb + i, 0, tile0 // n_tiles, 0, 0)
    kv_width = n_tiles * HEAD_DIM
    q_width = heads * HEAD_DIM
    in_specs = [pl.BlockSpec(blk, here),
                pl.BlockSpec((dec, 1, q_width), lambda i: (0, 0, (mixer * N_HEADS + head0) // heads)),
                pl.BlockSpec((None, dec, 1, kv_width), lambda i: (0, 0, 0, tile0 // n_tiles)),
                pl.BlockSpec((None, dec, 1, kv_width), lambda i: (1, 0, 0, tile0 // n_tiles)),
                _resident(kv_new_t.shape), _resident((heads, window))]
    args = [cache_t, q, kv_new, kv_new, kv_new_t, bias_rows[head0:head0 + heads]]
    aliases = {}
    if prev is not None:
        in_specs.append(pl.BlockSpec(memory_space=pl.ANY))
        args.append(prev)
        aliases = {len(args) - 1: 0}
    return _Job(
        body=functools.partial(_cache_stream_body, nb=nb, n0=n0, tiles=n_tiles, tile0=tile0, heads=heads,
                               head0=head0),
        steps=count // nb,
        in_specs=in_specs,
        out_specs=[pl.BlockSpec(blk, here),
                   pl.BlockSpec((count, 1, q_width), lambda i: (0, 0, 0)),
                   pl.BlockSpec((count, 1, 2 * LANES), lambda i: (0, 0, 0))],
        out_shape=[jax.ShapeDtypeStruct(cache_t.shape, F32),
                   jax.ShapeDtypeStruct((count, 1, q_width), F32),
                   jax.ShapeDtypeStruct((count, 1, 2 * LANES), F32)],
        scratch_shapes=[],
        args=args,
        name="cache_stream",
        aliases=aliases,
        phases=nb * (1 + 2 * n_tiles))


def _sample_merge_body(x_ref, ng_ref, wg_ref, pa_ref, pb_ref, wo_ref, e_ref, sink_ref,
                       o1_ref, l1_ref, o2_ref, l2_ref, o3_ref, l3_ref, ob_ref, lb_ref, y_ref):
    def store(y):
        y_ref[...] = y

    for _ in _merge_math(store, x_ref[...], ng_ref, wg_ref, pa_ref, pb_ref, wo_ref, e_ref, sink_ref,
                         [o1_ref[...], o2_ref[...], o3_ref[...]],
                         [l1_ref[...], l2_ref[...], l3_ref[...]], ob_ref[...], lb_ref[...]):
        pass


def _sample_merge(x, ng, w_gates, p_a, p_b, w_o, expand, sink_row, o_list, ml_list):
    args = [x, ng, w_gates, p_a, p_b, w_o, expand, sink_row]
    for o, l in zip(o_list, ml_list):
        args += [o, l]
    return pl.pallas_call(
        _sample_merge_body,
        grid=(1,),
        in_specs=[_whole(a.shape) for a in args],
        out_specs=_whole(x.shape),
        out_shape=jax.ShapeDtypeStruct(x.shape, F32),
        compiler_params=_params(dimension_semantics=("arbitrary",)),
        name="sample_merge",
    )(*args)


def _buffer_bias_rows(window, dilation):
    r = np.arange(window)
    steps = (window - r) // dilation
    slopes = np.exp2(-np.arange(1, N_HEADS + 1, dtype=np.float32))
    bias = -slopes[:, None] * steps[None].astype(np.float32)
    return np.where((r % dilation == 0)[None], bias * LOG2E, np.float32(NEG)).astype(np.float32)


def _band_bias_table():
    q = np.arange(BLOCK)[:, None]
    kk = np.arange(2 * BLOCK)[None, :]
    delta = q + BLOCK - kk
    valid = (delta >= 0) & (delta <= BLOCK)
    slopes = np.exp2(-np.arange(1, N_HEADS + 1, dtype=np.float32))
    bias = -slopes[:, None, None] * delta[None].astype(np.float32)
    normal = np.where(valid[None], bias, np.float32(NEG))
    first = np.where((valid & (kk >= BLOCK))[None], bias, np.float32(NEG))
    return (np.stack([first, normal]) * LOG2E).astype(np.float32)


def _block_diag_ones():
    i = np.arange(MXU_WIDTH)
    return (i[:, None] // HEAD_DIM == i[None, :] // HEAD_DIM).astype(np.float32)


def _head_expand_matrix():
    e = np.zeros((LANES, A_WIDTH), np.float32)
    for pos in range(N_HEADS):
        e[_stat_lane(pos), pos * HEAD_DIM:(pos + 1) * HEAD_DIM] = 1.0
    return e


def _b_column_order():
    return np.concatenate([np.arange(HEAD_DIM) + HEAD_DIM * hd for hd in B_HEAD_ORDER])


def _prepare_weights(norm_gain, w_in, qk_norm_a, qk_norm_b, b_sinks, w_branch_a, w_branch_b, w_out):
    w = w_in[0]
    order = _b_column_order()
    b_q0 = 9 * A_WIDTH
    w_qkv = jnp.concatenate([w[:, :b_q0], w[:, b_q0:b_q0 + A_WIDTH][:, order],
                             w[:, b_q0 + A_WIDTH:QKV_WIDTH]], axis=1).astype(BF16)
    g0 = QKV_WIDTH
    w_gates = jnp.concatenate([w[:, g0:g0 + A_WIDTH], w[:, g0 + A_WIDTH:g0 + 2 * A_WIDTH][:, order],
                               w[:, g0 + 2 * A_WIDTH:]], axis=1).astype(BF16)
    scale = HEAD_DIM ** -0.5 * LOG2E
    rows = []
    for g in range(3):
        rows.append(jnp.tile(qk_norm_a[0, g, 0], N_HEADS) * scale)
        rows.append(jnp.tile(qk_norm_a[0, g, 1], N_HEADS))
    rows.append(jnp.tile(qk_norm_b[0, 0], N_HEADS) * scale)
    rows.append(jnp.tile(qk_norm_b[0, 1], N_HEADS))
    gains = jnp.stack(rows).astype(F32)
    lanes = np.array([_stat_lane(pos) for pos in range(N_HEADS)])
    sink_row = jnp.zeros((1, LANES), F32).at[0, lanes].set(b_sinks[0][np.array(B_HEAD_ORDER)].astype(F32) * LOG2E)
    return dict(ng=norm_gain.astype(F32), w_qkv=w_qkv, w_gates=w_gates, gains=gains, sink_row=sink_row,
                p_a=w_branch_a[0].astype(BF16), p_b=w_branch_b[0][order].astype(BF16),
                w_o=w_out[0].astype(BF16))


def _cache_out(t):
    nb, two, width, rows = t.shape
    t = t.reshape(nb, two, width // HEAD_DIM, HEAD_DIM, rows)
    return jnp.transpose(t, (0, 4, 1, 2, 3))[None]


STREAM_BATCH = (2, 1, 1, 2)


def _layer(x_prompt, x_sample, caches, wts, consts):
    nb, seq, _ = x_prompt.shape
    dec = x_sample.shape[0]
    x2 = x_sample.reshape(dec, D_MODEL)
    q_s, kv_a, kv_b, kvt_a, kvt_b = _sample_proj(x2, wts["ng"], wts["w_qkv"], consts["bd"], wts["gains"])

    def stream_job(mixer, **part):
        cache_t = jnp.transpose(caches[mixer][0], (0, 2, 3, 4, 1))
        kv_new, kv_new_t = (kv_b, kvt_b) if mixer == 3 else (kv_a[mixer], kvt_a[mixer])
        return _cache_stream_job(cache_t, q_s[:, None, :], mixer, kv_new[:, :, None, :], kv_new_t,
                                 consts["rows"][mixer], STREAM_BATCH[mixer], **part)

    def attention_job(g):
        d = A_DILATIONS[g]
        shape = (nb * d, seq // d, A_WIDTH)
        return _attention_job(*(a.reshape(shape) for a in qkv[3 * g:3 * g + 3]), consts["bias"], NATURAL_ORDER)

    qkv, a3 = _run_pair(
        _prompt_proj_job(x_prompt, wts["ng"], wts["w_qkv"], consts["bd"], wts["gains"]), stream_job(2),
        interleave=False)
    prompt_caches = [_cache_out(c) for c in
                     _prompt_cache(x_prompt, wts["ng"], wts["w_qkv"], consts["bd"], wts["gains"])]
    attn = [None] * 3
    attn[0], a1 = _run_pair(attention_job(0), stream_job(0), interleave=False)
    attn[1] = _run(attention_job(1))
    attn[2] = _run(attention_job(2))
    (o_b, ml_b), b = _run_pair(
        _attention_job(qkv[9], qkv[10], qkv[11], consts["bias"], B_HEAD_ORDER), stream_job(3),
        interleave=False)
    o_list, ml_list = [a[0] for a in attn], [a[1] for a in attn]
    (y_prompt,), a2 = _run_pair(
        _prompt_merge_job(x_prompt, wts["ng"], wts["w_gates"], wts["p_a"], wts["p_b"], wts["w_o"],
                          consts["expand"], wts["sink_row"], o_list, ml_list, o_b, ml_b), stream_job(1),
        interleave=True)

    streams = (a1, a2, a3, b)
    rows = lambda a: a.reshape(a.shape[0], a.shape[2])
    sample_caches = [jnp.transpose(s[0], (0, 4, 1, 2, 3))[None] for s in streams]
    y_sample = _sample_merge(x2, wts["ng"], wts["w_gates"], wts["p_a"], wts["p_b"], wts["w_o"],
                             consts["expand"], wts["sink_row"],
                             [rows(s[1]) for s in streams], [rows(s[2]) for s in streams])
    return y_prompt, y_sample.reshape(dec, 1, D_MODEL), prompt_caches, sample_caches


def _constants():
    assert all(B_HEAD_ORDER[pos] // (N_HEADS // B_KV_HEADS) == pos % B_KV_HEADS for pos in range(N_HEADS))
    rows = [jnp.asarray(_buffer_bias_rows(w, d)) for w, d in zip(A_WINDOWS, A_DILATIONS)]
    rows.append(jnp.asarray(_buffer_bias_rows(B_WINDOW, 1)[np.array(B_HEAD_ORDER)]))
    return dict(bias=jnp.asarray(_band_bias_table()), bd=jnp.asarray(_block_diag_ones(), BF16),
                expand=jnp.asarray(_head_expand_matrix(), BF16), rows=rows)


def kernel(x_prompt, x_sample, cache_a1_kv, cache_a2_kv, cache_a3_kv, cache_b_kv, norm_gain, w_in,
           qk_norm_a, qk_norm_b, b_sinks, w_branch_a, w_branch_b, w_out):
    wts = _prepare_weights(norm_gain, w_in, qk_norm_a, qk_norm_b, b_sinks, w_branch_a, w_branch_b, w_out)
    consts = _constants()
    y_prompt, y_sample, pc, sc = _layer(x_prompt, x_sample, (cache_a1_kv, cache_a2_kv, cache_a3_kv, cache_b_kv),
                                        wts, consts)
    return (y_prompt, y_sample, pc[0], pc[1], pc[2], pc[3], sc[0], sc[1], sc[2], sc[3])
```

```python
import functools
from typing import Callable, NamedTuple, Optional, Sequence

import numpy as np
import jax
import jax.numpy as jnp
from jax import lax
from jax.experimental import pallas as pl
from jax.experimental.pallas import tpu as pltpu

F32 = jnp.float32
BF16 = jnp.bfloat16

D_MODEL = 1024
HEAD_DIM = 64
N_HEADS = 8
A_WINDOWS = (128, 512, 2048)
A_DILATIONS = (1, 4, 16)
B_WINDOW = 128
B_KV_HEADS = 2
A_WIDTH = N_HEADS * HEAD_DIM
B_KV_WIDTH = B_KV_HEADS * HEAD_DIM
QKV_WIDTH = 9 * A_WIDTH + A_WIDTH + 2 * B_KV_WIDTH
BLOCK = 128
Q_ROWS = 2048
NORM_EPS = 1e-6
NEG = -1e30
LOG2E = float(np.log2(np.e))
LANES = 128
MXU_WIDTH = 256
TOKEN_TILE = 256
CACHE_TILE = 512
ROW_TILE = 256
VMEM_LIMIT = 56 * 1024 * 1024
B_HEAD_ORDER = (0, 4, 1, 5, 2, 6, 3, 7)
NATURAL_ORDER = tuple(range(N_HEADS))

_CHUNKS = []
for _g in range(3):
    _CHUNKS += [(1536 * _g, 512, 2 * _g), (1536 * _g + 512, 512, 2 * _g + 1), (1536 * _g + 1024, 512, None)]
_CHUNKS += [(4608, 512, 6), (5120, 128, 7), (5248, 128, None)]


def _params(**kw):
    return pltpu.CompilerParams(vmem_limit_bytes=VMEM_LIMIT, **kw)


def _resident(shape):
    nd = len(shape)
    return pl.BlockSpec(shape, lambda *_: (0,) * nd, pipeline_mode=pl.Buffered(1))


def _whole(shape):
    nd = len(shape)
    return pl.BlockSpec(shape, lambda *_: (0,) * nd)


class _Job(NamedTuple):
    body: Callable
    steps: int
    in_specs: Sequence
    out_specs: Sequence
    out_shape: Sequence
    scratch_shapes: Sequence
    args: Sequence
    name: str
    aliases: Optional[dict] = None
    phases: int = 1


def _run(job):
    def body(*refs):
        for _ in job.body(*refs):
            pass

    return pl.pallas_call(
        body, grid=(job.steps,), in_specs=list(job.in_specs), out_specs=list(job.out_specs),
        out_shape=list(job.out_shape), scratch_shapes=list(job.scratch_shapes),
        input_output_aliases=dict(job.aliases or {}),
        compiler_params=_params(dimension_semantics=("arbitrary",)), name=job.name)(*job.args)


def _run_pair(a, b, interleave):
    if a.steps != b.steps:
        return _run(a), _run(b)
    n_in = (len(a.in_specs), len(b.in_specs))
    n_out = (len(a.out_specs), len(b.out_specs))
    n_scr = len(a.scratch_shapes)

    def body(*refs):
        ins, rest = refs[:sum(n_in)], refs[sum(n_in):]
        outs, scr = rest[:sum(n_out)], rest[sum(n_out):]
        gens = [a.body(*ins[:n_in[0]], *outs[:n_out[0]], *scr[:n_scr]),
                b.body(*ins[n_in[0]:], *outs[n_out[0]:], *scr[n_scr:])]
        credit = [0, 0]
        live = [True, True]
        while any(live):
            ahead = credit[0] * b.phases > credit[1] * a.phases if interleave else False
            k = 0 if (live[0] and not (live[1] and ahead)) else 1
            try:
                next(gens[k])
                credit[k] += 1
            except StopIteration:
                live[k] = False

    aliases = dict(a.aliases or {})
    aliases.update({n_in[0] + i: n_out[0] + o for i, o in (b.aliases or {}).items()})
    outs = pl.pallas_call(
        body, grid=(a.steps,), in_specs=list(a.in_specs) + list(b.in_specs),
        out_specs=list(a.out_specs) + list(b.out_specs), out_shape=list(a.out_shape) + list(b.out_shape),
        scratch_shapes=list(a.scratch_shapes) + list(b.scratch_shapes), input_output_aliases=aliases,
        compiler_params=_params(dimension_semantics=("arbitrary",)), name=a.name + "_" + b.name,
    )(*a.args, *b.args)
    return outs[:n_out[0]], outs[n_out[0]:]


def _rms_rows(x, gain_row):
    ms = jnp.mean(x * x, axis=-1, keepdims=True)
    return (x * lax.rsqrt(ms + NORM_EPS)) * gain_row


def _proj_chunk(h, w_ref, bd_ref, gain_ref, chunk):
    off, wd, gain_row = chunk
    parts = []
    for r0 in range(0, h.shape[0], ROW_TILE):
        acc = jnp.dot(h[r0:r0 + ROW_TILE], w_ref[:, off:off + wd], preferred_element_type=F32)
        if gain_row is not None:
            sq = (acc * acc).astype(BF16)
            step = min(wd, MXU_WIDTH)
            ss = jnp.concatenate(
                [jnp.dot(sq[:, c0:c0 + step], bd_ref[:step, :step], preferred_element_type=F32)
                 for c0 in range(0, wd, step)], axis=1)
            acc = (acc * lax.rsqrt(ss * (1.0 / HEAD_DIM) + NORM_EPS)) * gain_ref[gain_row:gain_row + 1, :wd]
        parts.append(acc)
    return parts[0] if len(parts) == 1 else jnp.concatenate(parts, axis=0)


def _sigmoid(z):
    return 1.0 / (1.0 + jnp.exp(-z))


def _prompt_proj_body(x_ref, ng_ref, w_ref, bd_ref, gain_ref, *refs, tt):
    outs = refs[:12]
    perm_ref = refs[12]
    h = _rms_rows(x_ref[0], ng_ref[...]).astype(BF16)
    for ci, chunk in enumerate(_CHUNKS):
        y = _proj_chunk(h, w_ref, bd_ref, gain_ref, chunk)
        mixer, role = divmod(ci, 3)
        o_ref = outs[ci]
        d = A_DILATIONS[mixer] if mixer < 3 else 1
        if mixer == 3:
            o_ref[0] = y.astype(BF16)
        elif d == 1:
            o_ref[0, 0] = y.astype(BF16)
        else:
            rows = tt // d
            for j in range(A_WIDTH // LANES):
                perm_ref[j] = y[:, j * LANES:(j + 1) * LANES]
            for r in range(d):
                for j in range(A_WIDTH // LANES):
                    o_ref[0, r, :, j * LANES:(j + 1) * LANES] = (
                        perm_ref[j, pl.ds(r, rows, stride=d), :].astype(BF16))
        yield


def _prompt_proj_job(x, ng, w_qkv, bd, gains):
    nb, seq, _ = x.shape
    tt = TOKEN_TILE
    tiles = seq // tt
    out_shape, out_specs = [], []
    for g in range(3):
        d = A_DILATIONS[g]
        for _ in range(3):
            out_shape.append(jax.ShapeDtypeStruct((nb, d, seq // d, A_WIDTH), BF16))
            out_specs.append(pl.BlockSpec((1, d, tt // d, A_WIDTH), lambda i: (i // tiles, 0, i % tiles, 0)))
    for wd in (A_WIDTH, B_KV_WIDTH, B_KV_WIDTH):
        out_shape.append(jax.ShapeDtypeStruct((nb, seq, wd), BF16))
        out_specs.append(pl.BlockSpec((1, tt, wd), lambda i: (i // tiles, i % tiles, 0)))
    return _Job(
        body=functools.partial(_prompt_proj_body, tt=tt),
        steps=nb * tiles,
        in_specs=[pl.BlockSpec((1, tt, D_MODEL), lambda i: (i // tiles, i % tiles, 0)),
                  _resident(ng.shape), _resident(w_qkv.shape), _resident(bd.shape), _resident(gains.shape)],
        out_specs=out_specs,
        out_shape=out_shape,
        scratch_shapes=[pltpu.VMEM((A_WIDTH // LANES, tt, LANES), F32)],
        args=(x, ng, w_qkv, bd, gains),
        name="prompt_proj",
        phases=len(_CHUNKS))


def _prompt_cache_body(x_ref, ng_ref, w_ref, bd_ref, gain_ref, c1_ref, c2_ref, c3_ref, cb_ref, y_scr, *, tt):
    j = pl.program_id(1)
    last = pl.num_programs(1) - 1
    h = _rms_rows(x_ref[0], ng_ref[...]).astype(BF16)

    def kv_t(rows, mixer, c):
        wd = _CHUNKS[3 * mixer + 1 + c][1]
        y_scr[:rows.shape[0], :wd] = _proj_chunk(rows, w_ref, bd_ref, gain_ref, _CHUNKS[3 * mixer + 1 + c])
        return y_scr[:rows.shape[0], :wd].T

    for c in range(2):
        c3_ref[0, c] = kv_t(h, 2, c)

    @pl.when(j == last)
    def _():
        tail = h[tt - BLOCK:]
        for c in range(2):
            c2_ref[0, c] = kv_t(h, 1, c)
            c1_ref[0, c] = kv_t(tail, 0, c)
            cb_ref[0, c] = kv_t(tail, 3, c)


def _prompt_cache(x, ng, w_qkv, bd, gains):
    nb, seq, _ = x.shape
    tt = CACHE_TILE
    assert A_WINDOWS == (BLOCK, tt, 4 * tt) and B_WINDOW == BLOCK
    n_tiles = A_WINDOWS[2] // tt
    first = seq // tt - n_tiles
    out_shape = [jax.ShapeDtypeStruct((nb, 2, A_WIDTH, w), F32) for w in A_WINDOWS]
    out_shape.append(jax.ShapeDtypeStruct((nb, 2, B_KV_WIDTH, B_WINDOW), F32))
    out_specs = [pl.BlockSpec((1, 2, A_WIDTH, BLOCK), lambda n, j: (n, 0, 0, 0)),
                 pl.BlockSpec((1, 2, A_WIDTH, tt), lambda n, j: (n, 0, 0, 0)),
                 pl.BlockSpec((1, 2, A_WIDTH, tt), lambda n, j: (n, 0, 0, j)),
                 pl.BlockSpec((1, 2, B_KV_WIDTH, BLOCK), lambda n, j: (n, 0, 0, 0))]
    return pl.pallas_call(
        functools.partial(_prompt_cache_body, tt=tt),
        grid=(nb, n_tiles),
        in_specs=[pl.BlockSpec((1, tt, D_MODEL), lambda n, j: (n, first + j, 0)),
                  _resident(ng.shape), _resident(w_qkv.shape), _resident(bd.shape), _resident(gains.shape)],
        out_specs=out_specs,
        out_shape=out_shape,
        scratch_shapes=[pltpu.VMEM((tt, A_WIDTH), F32)],
        compiler_params=_params(dimension_semantics=("arbitrary", "arbitrary")),
        name="prompt_cache",
    )(x, ng, w_qkv, bd, gains)


def _stat_lane(pos):
    return pos if pos % 2 else HEAD_DIM + pos


def _softmax_block(s, bias, v_ext):
    s = s + bias
    m = jnp.max(s, axis=-1, keepdims=True)
    p = jnp.exp2(s - m).astype(BF16)
    return m, jnp.dot(p, v_ext, preferred_element_type=F32)


def _attn_body(q_ref, kp_ref, kc_ref, vp_ref, vc_ref, bias_ref, o_ref, ml_ref,
               *, shared_kv, head_of_pos, per_seq):
    q_rows = q_ref.shape[1]
    n_blocks = q_rows // BLOCK
    first = jnp.minimum(pl.program_id(0) % per_seq, 1)
    lane = lax.broadcasted_iota(jnp.int32, (BLOCK, LANES), 1)
    lo = lane < HEAD_DIM
    lo_kv = lax.broadcasted_iota(jnp.int32, (BLOCK + q_rows, LANES), 1) < HEAD_DIM
    one = jnp.ones((), BF16)
    ml_ref[0, :, :LANES] = jnp.zeros((q_rows, LANES), F32)
    ml_ref[0, :, LANES:] = jnp.ones((q_rows, LANES), F32)

    def load_kv(cols):
        k_all = jnp.concatenate([kp_ref[0, :, cols], kc_ref[0, :, cols]], axis=0)
        v_all = jnp.concatenate([vp_ref[0, :, cols], vc_ref[0, :, cols]], axis=0)
        return k_all, (jnp.where(lo_kv, v_all, one), jnp.where(lo_kv, one, v_all))

    if shared_kv:
        k_all, v_ext = load_kv(slice(0, LANES))
    for i in range(N_HEADS // 2):
        cols = slice(i * LANES, (i + 1) * LANES)
        if not shared_kv:
            k_all, v_ext = load_kv(cols)
        for b in range(n_blocks):
            sel = first if b == 0 else 1
            keys = slice(b * BLOCK, (b + 2) * BLOCK)
            rows = slice(b * BLOCK, (b + 1) * BLOCK)
            qp = q_ref[0, rows, cols]
            zero = jnp.zeros_like(qp)
            q2 = jnp.concatenate([jnp.where(lo, qp, zero), jnp.where(lo, zero, qp)], axis=0)
            s2 = lax.dot_general(q2, k_all[keys], (((1,), (1,)), ((), ())), preferred_element_type=F32)
            m_pair, pv_pair = [], []
            for hh in range(2):
                m, pv = _softmax_block(s2[hh * BLOCK:(hh + 1) * BLOCK],
                                       bias_ref[sel, head_of_pos[2 * i + hh]], v_ext[hh][keys])
                m_pair.append(m)
                pv_pair.append(pv)
            o_ref[0, rows, cols] = jnp.where(lo, pv_pair[0], pv_pair[1])
            for hh in range(2):
                at = _stat_lane(2 * i + hh)
                ml_ref[0, rows, at:at + 1] = m_pair[hh]
                ml_ref[0, rows, LANES + at:LANES + at + 1] = pv_pair[hh][:, at:at + 1]
            yield


def _attention_job(q, k, v, bias, head_of_pos):
    ns, steps, _ = q.shape
    kv_width = k.shape[-1]
    q_rows = min(Q_ROWS, steps)
    ratio = q_rows // BLOCK
    per_seq = steps // q_rows
    here = lambda i: (i // per_seq, i % per_seq, 0)
    qspec = pl.BlockSpec((1, q_rows, A_WIDTH), here)
    cur = pl.BlockSpec((1, q_rows, kv_width), here)
    prev = pl.BlockSpec((1, BLOCK, kv_width),
                        lambda i: (i // per_seq, jnp.maximum((i % per_seq) * ratio - 1, 0), 0))
    return _Job(
        body=functools.partial(_attn_body, shared_kv=kv_width == LANES, head_of_pos=head_of_pos,
                               per_seq=per_seq),
        steps=ns * per_seq,
        in_specs=[qspec, prev, cur, prev, cur, _resident(bias.shape)],
        out_specs=[qspec, pl.BlockSpec((1, q_rows, 2 * LANES), here)],
        out_shape=[jax.ShapeDtypeStruct((ns, steps, A_WIDTH), F32),
                   jax.ShapeDtypeStruct((ns, steps, 2 * LANES), F32)],
        scratch_shapes=[],
        args=(q, k, k, v, v, bias),
        name="attention",
        phases=(N_HEADS // 2) * ratio)


def _head_expand(w, e_ref):
    return jnp.dot(w.astype(BF16), e_ref[...], preferred_element_type=F32)


MERGE_PHASES = 8


def _merge_math(store, x, ng_ref, wg_ref, pa_ref, pb_ref, wo_ref, e_ref, sink_ref, o_groups, ml_groups, o_b,
                ml_b):
    ms = [ml[:, :LANES] for ml in ml_groups]
    ls = [ml[:, LANES:] for ml in ml_groups]
    big = jnp.maximum(jnp.maximum(ms[0], ms[1]), ms[2])
    es = [jnp.exp2(m - big) for m in ms]
    inv = 1.0 / (es[0] * ls[0] + es[1] * ls[1] + es[2] * ls[2])
    o_a = None
    for e, o in zip(es, o_groups):
        term = _head_expand(e * inv, e_ref) * o
        o_a = term if o_a is None else o_a + term
    inv_b = 1.0 / (ml_b[:, LANES:] + jnp.exp2(sink_ref[...] - ml_b[:, :LANES]))
    o_b = _head_expand(inv_b, e_ref) * o_b
    yield
    h = _rms_rows(x, ng_ref[...]).astype(BF16)
    gate = []
    for c0, c1 in ((0, A_WIDTH), (A_WIDTH, 2 * A_WIDTH), (2 * A_WIDTH, 2 * A_WIDTH + D_MODEL),
                   (2 * A_WIDTH + D_MODEL, 2 * A_WIDTH + 2 * D_MODEL)):
        gate.append(jnp.dot(h, wg_ref[:, c0:c1], preferred_element_type=F32))
        yield
    z_a, z_b, g_a, g_b = gate
    a = (o_a * (z_a * _sigmoid(z_a))).astype(BF16)
    b = (o_b * (z_b * _sigmoid(z_b))).astype(BF16)
    mixed = _sigmoid(g_a) * jnp.dot(a, pa_ref[...], preferred_element_type=F32)
    yield
    mixed = mixed + _sigmoid(g_b) * jnp.dot(b, pb_ref[...], preferred_element_type=F32)
    yield
    store(x + jnp.dot(mixed.astype(BF16), wo_ref[...], preferred_element_type=F32))
    yield


def _prompt_merge_body(x_ref, ng_ref, wg_ref, pa_ref, pb_ref, wo_ref, e_ref, sink_ref,
                       o1_ref, l1_ref, o2_ref, l2_ref, o3_ref, l3_ref, ob_ref, lb_ref, y_ref,
                       o_scr, l_scr, *, tt):
    for gi, (o_ref, l_ref, d) in enumerate(((o2_ref, l2_ref, 4), (o3_ref, l3_ref, 16))):
        rows = tt // d
        for r in range(d):
            for j in range(2):
                l_scr[gi, j, pl.ds(r, rows, stride=d), :] = l_ref[0, r, :, j * LANES:(j + 1) * LANES]
            for j in range(A_WIDTH // LANES):
                o_scr[gi, j, pl.ds(r, rows, stride=d), :] = o_ref[0, r, :, j * LANES:(j + 1) * LANES]
        yield
    o_groups = [o1_ref[0, 0]] + [
        jnp.concatenate([o_scr[gi, j] for j in range(A_WIDTH // LANES)], axis=1) for gi in range(2)]
    ml_groups = [l1_ref[0, 0]] + [jnp.concatenate([l_scr[gi, 0], l_scr[gi, 1]], axis=1) for gi in range(2)]

    def store(y):
        y_ref[0] = y

    yield from _merge_math(store, x_ref[0], ng_ref, wg_ref, pa_ref, pb_ref, wo_ref, e_ref, sink_ref,
                           o_groups, ml_groups, ob_ref[0], lb_ref[0])


def _prompt_merge_job(x, ng, w_gates, p_a, p_b, w_o, expand, sink_row, o_list, ml_list, o_b, ml_b):
    nb, seq, _ = x.shape
    tt = TOKEN_TILE
    tiles = seq // tt
    tile3 = lambda i: (i // tiles, i % tiles, 0)
    tile4 = lambda i: (i // tiles, 0, i % tiles, 0)
    in_specs = [pl.BlockSpec((1, tt, D_MODEL), tile3)]
    in_specs += [_resident(a.shape) for a in (ng, w_gates, p_a, p_b, w_o, expand, sink_row)]
    args = [x, ng, w_gates, p_a, p_b, w_o, expand, sink_row]
    for g in range(3):
        d = A_DILATIONS[g]
        in_specs.append(pl.BlockSpec((1, d, tt // d, A_WIDTH), tile4))
        in_specs.append(pl.BlockSpec((1, d, tt // d, 2 * LANES), tile4))
        args += [o_list[g].reshape(nb, d, seq // d, A_WIDTH), ml_list[g].reshape(nb, d, seq // d, 2 * LANES)]
    in_specs.append(pl.BlockSpec((1, tt, A_WIDTH), tile3))
    in_specs.append(pl.BlockSpec((1, tt, 2 * LANES), tile3))
    args += [o_b, ml_b]
    return _Job(
        body=functools.partial(_prompt_merge_body, tt=tt),
        steps=nb * tiles,
        in_specs=in_specs,
        out_specs=[pl.BlockSpec((1, tt, D_MODEL), tile3)],
        out_shape=[jax.ShapeDtypeStruct((nb, seq, D_MODEL), F32)],
        scratch_shapes=[pltpu.VMEM((2, A_WIDTH // LANES, tt, LANES), F32),
                        pltpu.VMEM((2, 2, tt, LANES), F32)],
        args=args,
        name="prompt_merge",
        phases=MERGE_PHASES + 2)


def _sample_proj_body(x_ref, ng_ref, w_ref, bd_ref, gain_ref, q_ref, kva_ref, kvb_ref, kvta_ref, kvtb_ref,
                      y_scr):
    dec = x_ref.shape[0]
    h = _rms_rows(x_ref[...], ng_ref[...]).astype(BF16)
    kvta_ref[...] = jnp.zeros_like(kvta_ref)
    kvtb_ref[...] = jnp.zeros_like(kvtb_ref)
    for ci, chunk in enumerate(_CHUNKS):
        wd = chunk[1]
        y = _proj_chunk(h, w_ref, bd_ref, gain_ref, chunk)
        mixer, role = divmod(ci, 3)
        if role == 0:
            q_ref[:, mixer * A_WIDTH:(mixer + 1) * A_WIDTH] = y
            continue
        y_scr[:, :wd] = y
        y_t = y_scr[:, :wd].T
        if mixer < 3:
            kva_ref[mixer, role - 1] = y
            kvta_ref[mixer, role - 1, :, :dec] = y_t
        else:
            kvb_ref[role - 1] = y
            kvtb_ref[role - 1, :, :dec] = y_t


def _sample_proj(x, ng, w_qkv, bd, gains):
    dec = x.shape[0]
    assert dec <= LANES
    out_shape = [jax.ShapeDtypeStruct((dec, 4 * A_WIDTH), F32),
                 jax.ShapeDtypeStruct((3, 2, dec, A_WIDTH), F32),
                 jax.ShapeDtypeStruct((2, dec, B_KV_WIDTH), F32),
                 jax.ShapeDtypeStruct((3, 2, A_WIDTH, LANES), F32),
                 jax.ShapeDtypeStruct((2, B_KV_WIDTH, LANES), F32)]
    args = (x, ng, w_qkv, bd, gains)
    return pl.pallas_call(
        _sample_proj_body,
        grid=(1,),
        in_specs=[_whole(a.shape) for a in args],
        out_specs=[_whole(o.shape) for o in out_shape],
        out_shape=out_shape,
        scratch_shapes=[pltpu.VMEM((dec, A_WIDTH), F32)],
        compiler_params=_params(dimension_semantics=("arbitrary",)),
        name="sample_proj",
    )(*args)


def _cache_stream_body(c_ref, q_ref, kn_ref, vn_ref, kvt_ref, bias_ref, *rest, nb, n0, tiles, tile0, heads, head0):
    out_ref, o_ref, ml_ref = rest[-3:]
    window = c_ref.shape[-1]
    width = heads * HEAD_DIM
    reps = heads // tiles
    step = pl.program_id(0)
    head_lane = lax.broadcasted_iota(jnp.int32, (heads, width), 1) // HEAD_DIM
    head_row = lax.broadcasted_iota(jnp.int32, (heads, width), 0)
    own = head_lane == head_row
    spos = lax.broadcasted_iota(jnp.int32, (heads, LANES), 0) + head0
    slane = lax.broadcasted_iota(jnp.int32, (heads, LANES), 1)
    stat = slane == jnp.where(spos % 2 == 1, spos, spos + HEAD_DIM)
    used = jnp.sum(stat.astype(F32), axis=0, keepdims=True) > 0.0
    lane_w = lax.broadcasted_iota(jnp.int32, (HEAD_DIM, window), 1)

    def stacked(tiles_):
        return jnp.concatenate([tiles_[pos % tiles] for pos in range(heads)], axis=0)

    def tiled(row):
        return row if reps == 1 else jnp.concatenate([row] * reps, axis=1)

    for bi in range(nb):
        local = step * nb + bi
        n = n0 + local
        q_row = q_ref[n]
        kn_row = tiled(kn_ref[n])
        vn_row = tiled(vn_ref[n])
        q_bd = jnp.where(own, q_row, 0.0)
        k_all = stacked([c_ref[bi, 0, g].astype(BF16) for g in range(tiles)])
        v_all = stacked([c_ref[bi, 1, g].astype(BF16) for g in range(tiles)])
        s = jnp.dot(q_bd.astype(BF16), k_all, preferred_element_type=F32) + bias_ref[...]
        s_new = jnp.sum(q_bd * kn_row, axis=1, keepdims=True)
        m = jnp.maximum(jnp.max(s, axis=1, keepdims=True), s_new)
        p = jnp.exp2(s - m)
        p_new = jnp.exp2(s_new - m)
        l = jnp.sum(p, axis=1, keepdims=True) + p_new
        o_t = lax.dot_general(p.astype(BF16), v_all, (((1,), (1,)), ((), ())), preferred_element_type=F32)
        o_ref[local] = jnp.sum(jnp.where(own, o_t + p_new * vn_row, 0.0), axis=0, keepdims=True)
        ml_ref[local, :, :LANES] = jnp.sum(jnp.where(stat, m, 0.0), axis=0, keepdims=True)
        ml_ref[local, :, LANES:] = jnp.where(
            used, jnp.sum(jnp.where(stat, l, 0.0), axis=0, keepdims=True), 1.0)
        yield
        for c in range(2):
            for g in range(tiles):
                r0 = (tile0 + g) * HEAD_DIM
                new = pltpu.roll(kvt_ref[c, r0:r0 + HEAD_DIM, :], LANES - 1 - n, axis=1)
                if window > LANES:
                    new = jnp.concatenate([new] * (window // LANES), axis=1)
                shifted = pltpu.roll(c_ref[bi, c, g], window - 1, axis=1)
                out_ref[bi, c, g] = jnp.where(lane_w == window - 1, new, shifted)
                if (c * tiles + g) % STREAM_TILES_PER_PHASE == STREAM_TILES_PER_PHASE - 1:
                    yield


def _cache_stream_job(cache_t, q, mixer, kv_new, kv_new_t, bias_rows, nb, *, batch=None, tiles=None,
                      prev=None):
    dec, _, kv_heads, _, window = cache_t.shape
    n0, count = batch or (0, dec)
    tile0, n_tiles = tiles or (0, kv_heads)
    assert tiles is None or kv_heads == N_HEADS
    heads = n_tiles * (N_HEADS // kv_heads)
    head0 = tile0 * (N_HEADS // kv_heads)
    nb = min(nb, count)
    blk = (nb, 2, n_tiles, HEAD_DIM, window)
    here = lambda i: (n0 // nb + i, 0, tile0 // n_tiles, 0, 0)
    kv_width = n_tiles * HEAD_DIM
    q_width = heads * HEAD_DIM
    in_specs = [pl.BlockSpec(blk, here),
                pl.BlockSpec((dec, 1, q_width), lambda i: (0, 0, (mixer * N_HEADS + head0) // heads)),
                pl.BlockSpec((None, dec, 1, kv_width), lambda i: (0, 0, 0, tile0 // n_tiles)),
                pl.BlockSpec((None, dec, 1, kv_width), lambda i: (1, 0, 0, tile0 // n_tiles)),
                _resident(kv_new_t.shape), _resident((heads, window))]
    args = [cache_t, q, kv_new, kv_new, kv_new_t, bias_rows[head0:head0 + heads]]
    aliases = {}
    if prev is not None:
        in_specs.append(pl.BlockSpec(memory_space=pl.ANY))
        args.append(prev)
        aliases = {len(args) - 1: 0}
    return _Job(
        body=functools.partial(_cache_stream_body, nb=nb, n0=n0, tiles=n_tiles, tile0=tile0, heads=heads,
                               head0=head0),
        steps=count // nb,
        in_specs=in_specs,
        out_specs=[pl.BlockSpec(blk, here),
                   pl.BlockSpec((count, 1, q_width), lambda i: (0, 0, 0)),
                   pl.BlockSpec((count, 1, 2 * LANES), lambda i: (0, 0, 0))],
        out_shape=[jax.ShapeDtypeStruct(cache_t.shape, F32),
                   jax.ShapeDtypeStruct((count, 1, q_width), F32),
                   jax.ShapeDtypeStruct((count, 1, 2 * LANES), F32)],
        scratch_shapes=[],
        args=args,
        name="cache_stream",
        aliases=aliases,
        phases=nb * (1 + 2 * n_tiles // STREAM_TILES_PER_PHASE))


def _sample_merge_body(x_ref, ng_ref, wg_ref, pa_ref, pb_ref, wo_ref, e_ref, sink_ref,
                       o1_ref, l1_ref, o2_ref, l2_ref, o3_ref, l3_ref, ob_ref, lb_ref, y_ref):
    def store(y):
        y_ref[...] = y

    for _ in _merge_math(store, x_ref[...], ng_ref, wg_ref, pa_ref, pb_ref, wo_ref, e_ref, sink_ref,
                         [o1_ref[...], o2_ref[...], o3_ref[...]],
                         [l1_ref[...], l2_ref[...], l3_ref[...]], ob_ref[...], lb_ref[...]):
        pass


def _sample_merge(x, ng, w_gates, p_a, p_b, w_o, expand, sink_row, o_list, ml_list):
    args = [x, ng, w_gates, p_a, p_b, w_o, expand, sink_row]
    for o, l in zip(o_list, ml_list):
        args += [o, l]
    return pl.pallas_call(
        _sample_merge_body,
        grid=(1,),
        in_specs=[_whole(a.shape) for a in args],
        out_specs=_whole(x.shape),
        out_shape=jax.ShapeDtypeStruct(x.shape, F32),
        compiler_params=_params(dimension_semantics=("arbitrary",)),
        name="sample_merge",
    )(*args)


def _buffer_bias_rows(window, dilation):
    r = np.arange(window)
    steps = (window - r) // dilation
    slopes = np.exp2(-np.arange(1, N_HEADS + 1, dtype=np.float32))
    bias = -slopes[:, None] * steps[None].astype(np.float32)
    return np.where((r % dilation == 0)[None], bias * LOG2E, np.float32(NEG)).astype(np.float32)


def _band_bias_table():
    q = np.arange(BLOCK)[:, None]
    kk = np.arange(2 * BLOCK)[None, :]
    delta = q + BLOCK - kk
    valid = (delta >= 0) & (delta <= BLOCK)
    slopes = np.exp2(-np.arange(1, N_HEADS + 1, dtype=np.float32))
    bias = -slopes[:, None, None] * delta[None].astype(np.float32)
    normal = np.where(valid[None], bias, np.float32(NEG))
    first = np.where((valid & (kk >= BLOCK))[None], bias, np.float32(NEG))
    return (np.stack([first, normal]) * LOG2E).astype(np.float32)


def _block_diag_ones():
    i = np.arange(MXU_WIDTH)
    return (i[:, None] // HEAD_DIM == i[None, :] // HEAD_DIM).astype(np.float32)


def _head_expand_matrix():
    e = np.zeros((LANES, A_WIDTH), np.float32)
    for pos in range(N_HEADS):
        e[_stat_lane(pos), pos * HEAD_DIM:(pos + 1) * HEAD_DIM] = 1.0
    return e


def _b_column_order():
    return np.concatenate([np.arange(HEAD_DIM) + HEAD_DIM * hd for hd in B_HEAD_ORDER])


def _prepare_weights(norm_gain, w_in, qk_norm_a, qk_norm_b, b_sinks, w_branch_a, w_branch_b, w_out):
    w = w_in[0]
    order = _b_column_order()
    b_q0 = 9 * A_WIDTH
    w_qkv = jnp.concatenate([w[:, :b_q0], w[:, b_q0:b_q0 + A_WIDTH][:, order],
                             w[:, b_q0 + A_WIDTH:QKV_WIDTH]], axis=1).astype(BF16)
    g0 = QKV_WIDTH
    w_gates = jnp.concatenate([w[:, g0:g0 + A_WIDTH], w[:, g0 + A_WIDTH:g0 + 2 * A_WIDTH][:, order],
                               w[:, g0 + 2 * A_WIDTH:]], axis=1).astype(BF16)
    scale = HEAD_DIM ** -0.5 * LOG2E
    rows = []
    for g in range(3):
        rows.append(jnp.tile(qk_norm_a[0, g, 0], N_HEADS) * scale)
        rows.append(jnp.tile(qk_norm_a[0, g, 1], N_HEADS))
    rows.append(jnp.tile(qk_norm_b[0, 0], N_HEADS) * scale)
    rows.append(jnp.tile(qk_norm_b[0, 1], N_HEADS))
    gains = jnp.stack(rows).astype(F32)
    lanes = np.array([_stat_lane(pos) for pos in range(N_HEADS)])
    sink_row = jnp.zeros((1, LANES), F32).at[0, lanes].set(b_sinks[0][np.array(B_HEAD_ORDER)].astype(F32) * LOG2E)
    return dict(ng=norm_gain.astype(F32), w_qkv=w_qkv, w_gates=w_gates, gains=gains, sink_row=sink_row,
                p_a=w_branch_a[0].astype(BF16), p_b=w_branch_b[0][order].astype(BF16),
                w_o=w_out[0].astype(BF16))


def _cache_out(t):
    nb, two, width, rows = t.shape
    t = t.reshape(nb, two, width // HEAD_DIM, HEAD_DIM, rows)
    return jnp.transpose(t, (0, 4, 1, 2, 3))[None]


STREAM_BATCH = (2, 1, 1, 8)
STREAM_TILES_PER_PHASE = 4


def _layer(x_prompt, x_sample, caches, wts, consts):
    nb, seq, _ = x_prompt.shape
    dec = x_sample.shape[0]
    x2 = x_sample.reshape(dec, D_MODEL)
    q_s, kv_a, kv_b, kvt_a, kvt_b = _sample_proj(x2, wts["ng"], wts["w_qkv"], consts["bd"], wts["gains"])

    def stream_job(mixer, **part):
        cache_t = jnp.transpose(caches[mixer][0], (0, 2, 3, 4, 1))
        kv_new, kv_new_t = (kv_b, kvt_b) if mixer == 3 else (kv_a[mixer], kvt_a[mixer])
        return _cache_stream_job(cache_t, q_s[:, None, :], mixer, kv_new[:, :, None, :], kv_new_t,
                                 consts["rows"][mixer], STREAM_BATCH[mixer], **part)

    def attention_job(g):
        d = A_DILATIONS[g]
        shape = (nb * d, seq // d, A_WIDTH)
        return _attention_job(*(a.reshape(shape) for a in qkv[3 * g:3 * g + 3]), consts["bias"], NATURAL_ORDER)

    qkv, a3 = _run_pair(
        _prompt_proj_job(x_prompt, wts["ng"], wts["w_qkv"], consts["bd"], wts["gains"]), stream_job(2),
        interleave=False)
    prompt_caches = [_cache_out(c) for c in
                     _prompt_cache(x_prompt, wts["ng"], wts["w_qkv"], consts["bd"], wts["gains"])]
    attn = [None] * 3
    attn[0] = _run(attention_job(0))
    attn[1] = _run(attention_job(1))
    attn[2], a1 = _run_pair(attention_job(2), stream_job(0), interleave=False)
    (o_b, ml_b), b = _run_pair(
        _attention_job(qkv[9], qkv[10], qkv[11], consts["bias"], B_HEAD_ORDER), stream_job(3),
        interleave=False)
    o_list, ml_list = [a[0] for a in attn], [a[1] for a in attn]
    (y_prompt,), a2 = _run_pair(
        _prompt_merge_job(x_prompt, wts["ng"], wts["w_gates"], wts["p_a"], wts["p_b"], wts["w_o"],
                          consts["expand"], wts["sink_row"], o_list, ml_list, o_b, ml_b), stream_job(1),
        interleave=True)

    streams = (a1, a2, a3, b)
    rows = lambda a: a.reshape(a.shape[0], a.shape[2])
    sample_caches = [jnp.transpose(s[0], (0, 4, 1, 2, 3))[None] for s in streams]
    y_sample = _sample_merge(x2, wts["ng"], wts["w_gates"], wts["p_a"], wts["p_b"], wts["w_o"],
                             consts["expand"], wts["sink_row"],
                             [rows(s[1]) for s in streams], [rows(s[2]) for s in streams])
    return y_prompt, y_sample.reshape(dec, 1, D_MODEL), prompt_caches, sample_caches


def _constants():
    assert all(B_HEAD_ORDER[pos] // (N_HEADS // B_KV_HEADS) == pos % B_KV_HEADS for pos in range(N_HEADS))
    rows = [jnp.asarray(_buffer_bias_rows(w, d)) for w, d in zip(A_WINDOWS, A_DILATIONS)]
    rows.append(jnp.asarray(_buffer_bias_rows(B_WINDOW, 1)[np.array(B_HEAD_ORDER)]))
    return dict(bias=jnp.asarray(_band_bias_table()), bd=jnp.asarray(_block_diag_ones(), BF16),
                expand=jnp.asarray(_head_expand_matrix(), BF16), rows=rows)


def kernel(x_prompt, x_sample, cache_a1_kv, cache_a2_kv, cache_a3_kv, cache_b_kv, norm_gain, w_in,
           qk_norm_a, qk_norm_b, b_sinks, w_branch_a, w_branch_b, w_out):
    wts = _prepare_weights(norm_gain, w_in, qk_norm_a, qk_norm_b, b_sinks, w_branch_a, w_branch_b, w_out)
    consts = _constants()
    y_prompt, y_sample, pc, sc = _layer(x_prompt, x_sample, (cache_a1_kv, cache_a2_kv, cache_a3_kv, cache_b_kv),
                                        wts, consts)
    return (y_prompt, y_sample, pc[0], pc[1], pc[2], pc[3], sc[0], sc[1], sc[2], sc[3])
```

```python
import functools
from typing import Callable, NamedTuple, Optional, Sequence

import numpy as np
import jax
import jax.numpy as jnp
from jax import lax
from jax.experimental import pallas as pl
from jax.experimental.pallas import tpu as pltpu

F32 = jnp.float32
BF16 = jnp.bfloat16

D_MODEL = 1024
HEAD_DIM = 64
N_HEADS = 8
A_WINDOWS = (128, 512, 2048)
A_DILATIONS = (1, 4, 16)
B_WINDOW = 128
B_KV_HEADS = 2
A_WIDTH = N_HEADS * HEAD_DIM
B_KV_WIDTH = B_KV_HEADS * HEAD_DIM
QKV_WIDTH = 9 * A_WIDTH + A_WIDTH + 2 * B_KV_WIDTH
BLOCK = 128
Q_ROWS = 2048
NORM_EPS = 1e-6
NEG = -1e30
LOG2E = float(np.log2(np.e))
LANES = 128
MXU_WIDTH = 256
TOKEN_TILE = 256
CACHE_TILE = 512
ROW_TILE = 256
VMEM_LIMIT = 56 * 1024 * 1024
B_HEAD_ORDER = (0, 4, 1, 5, 2, 6, 3, 7)
NATURAL_ORDER = tuple(range(N_HEADS))

_CHUNKS = []
for _g in range(3):
    _CHUNKS += [(1536 * _g, 512, 2 * _g), (1536 * _g + 512, 512, 2 * _g + 1), (1536 * _g + 1024, 512, None)]
_CHUNKS += [(4608, 512, 6), (5120, 128, 7), (5248, 128, None)]


def _params(**kw):
    return pltpu.CompilerParams(vmem_limit_bytes=VMEM_LIMIT, **kw)


def _resident(shape):
    nd = len(shape)
    return pl.BlockSpec(shape, lambda *_: (0,) * nd, pipeline_mode=pl.Buffered(1))


def _whole(shape):
    nd = len(shape)
    return pl.BlockSpec(shape, lambda *_: (0,) * nd)


class _Job(NamedTuple):
    body: Callable
    steps: int
    in_specs: Sequence
    out_specs: Sequence
    out_shape: Sequence
    scratch_shapes: Sequence
    args: Sequence
    name: str
    phases: int = 1


def _run(job):
    def body(*refs):
        for _ in job.body(*refs):
            pass

    return pl.pallas_call(
        body, grid=(job.steps,), in_specs=list(job.in_specs), out_specs=list(job.out_specs),
        out_shape=list(job.out_shape), scratch_shapes=list(job.scratch_shapes),
        compiler_params=_params(dimension_semantics=("arbitrary",)), name=job.name)(*job.args)


def _run_pair(a, b, interleave):
    if a.steps != b.steps:
        return _run(a), _run(b)
    n_in = (len(a.in_specs), len(b.in_specs))
    n_out = (len(a.out_specs), len(b.out_specs))
    n_scr = len(a.scratch_shapes)

    def body(*refs):
        ins, rest = refs[:sum(n_in)], refs[sum(n_in):]
        outs, scr = rest[:sum(n_out)], rest[sum(n_out):]
        gens = [a.body(*ins[:n_in[0]], *outs[:n_out[0]], *scr[:n_scr]),
                b.body(*ins[n_in[0]:], *outs[n_out[0]:], *scr[n_scr:])]
        credit = [0, 0]
        live = [True, True]
        while any(live):
            ahead = credit[0] * b.phases > credit[1] * a.phases if interleave else False
            k = 0 if (live[0] and not (live[1] and ahead)) else 1
            try:
                next(gens[k])
                credit[k] += 1
            except StopIteration:
                live[k] = False

    outs = pl.pallas_call(
        body, grid=(a.steps,), in_specs=list(a.in_specs) + list(b.in_specs),
        out_specs=list(a.out_specs) + list(b.out_specs), out_shape=list(a.out_shape) + list(b.out_shape),
        scratch_shapes=list(a.scratch_shapes) + list(b.scratch_shapes),
        compiler_params=_params(dimension_semantics=("arbitrary",)), name=a.name + "_" + b.name,
    )(*a.args, *b.args)
    return outs[:n_out[0]], outs[n_out[0]:]


def _rms_rows(x, gain_row):
    ms = jnp.mean(x * x, axis=-1, keepdims=True)
    return (x * lax.rsqrt(ms + NORM_EPS)) * gain_row


def _proj_chunk(h, w_ref, bd_ref, gain_ref, chunk):
    off, wd, gain_row = chunk
    parts = []
    for r0 in range(0, h.shape[0], ROW_TILE):
        acc = jnp.dot(h[r0:r0 + ROW_TILE], w_ref[:, off:off + wd], preferred_element_type=F32)
        if gain_row is not None:
            sq = (acc * acc).astype(BF16)
            step = min(wd, MXU_WIDTH)
            ss = jnp.concatenate(
                [jnp.dot(sq[:, c0:c0 + step], bd_ref[:step, :step], preferred_element_type=F32)
                 for c0 in range(0, wd, step)], axis=1)
            acc = (acc * lax.rsqrt(ss * (1.0 / HEAD_DIM) + NORM_EPS)) * gain_ref[gain_row:gain_row + 1, :wd]
        parts.append(acc)
    return parts[0] if len(parts) == 1 else jnp.concatenate(parts, axis=0)


def _sigmoid(z):
    return 1.0 / (1.0 + jnp.exp(-z))


def _prompt_proj_body(x_ref, ng_ref, w_ref, bd_ref, gain_ref, *refs, tt):
    outs = refs[:12]
    perm_ref = refs[12]
    h = _rms_rows(x_ref[0], ng_ref[...]).astype(BF16)
    for ci, chunk in enumerate(_CHUNKS):
        y = _proj_chunk(h, w_ref, bd_ref, gain_ref, chunk)
        mixer, role = divmod(ci, 3)
        o_ref = outs[ci]
        d = A_DILATIONS[mixer] if mixer < 3 else 1
        if mixer == 3:
            o_ref[0] = y.astype(BF16)
        elif d == 1:
            o_ref[0, 0] = y.astype(BF16)
        else:
            rows = tt // d
            for j in range(A_WIDTH // LANES):
                perm_ref[j] = y[:, j * LANES:(j + 1) * LANES]
            for r in range(d):
                for j in range(A_WIDTH // LANES):
                    o_ref[0, r, :, j * LANES:(j + 1) * LANES] = (
                        perm_ref[j, pl.ds(r, rows, stride=d), :].astype(BF16))
        yield


def _prompt_proj_job(x, ng, w_qkv, bd, gains):
    nb, seq, _ = x.shape
    tt = TOKEN_TILE
    tiles = seq // tt
    out_shape, out_specs = [], []
    for g in range(3):
        d = A_DILATIONS[g]
        for _ in range(3):
            out_shape.append(jax.ShapeDtypeStruct((nb, d, seq // d, A_WIDTH), BF16))
            out_specs.append(pl.BlockSpec((1, d, tt // d, A_WIDTH), lambda i: (i // tiles, 0, i % tiles, 0)))
    for wd in (A_WIDTH, B_KV_WIDTH, B_KV_WIDTH):
        out_shape.append(jax.ShapeDtypeStruct((nb, seq, wd), BF16))
        out_specs.append(pl.BlockSpec((1, tt, wd), lambda i: (i // tiles, i % tiles, 0)))
    return _Job(
        body=functools.partial(_prompt_proj_body, tt=tt),
        steps=nb * tiles,
        in_specs=[pl.BlockSpec((1, tt, D_MODEL), lambda i: (i // tiles, i % tiles, 0)),
                  _resident(ng.shape), _resident((D_MODEL, QKV_WIDTH)), _resident(bd.shape),
                  _resident(gains.shape)],
        out_specs=out_specs,
        out_shape=out_shape,
        scratch_shapes=[pltpu.VMEM((A_WIDTH // LANES, tt, LANES), F32)],
        args=(x, ng, w_qkv, bd, gains),
        name="prompt_proj",
        phases=len(_CHUNKS))


def _prompt_cache_body(x_ref, ng_ref, w_ref, bd_ref, gain_ref, c1_ref, c2_ref, c3_ref, cb_ref, y_scr, *, tt):
    j = pl.program_id(1)
    last = pl.num_programs(1) - 1
    h = _rms_rows(x_ref[0], ng_ref[...]).astype(BF16)

    def kv_t(rows, mixer, c):
        wd = _CHUNKS[3 * mixer + 1 + c][1]
        y_scr[:rows.shape[0], :wd] = _proj_chunk(rows, w_ref, bd_ref, gain_ref, _CHUNKS[3 * mixer + 1 + c])
        return y_scr[:rows.shape[0], :wd].T

    for c in range(2):
        c3_ref[0, c] = kv_t(h, 2, c)

    @pl.when(j == last)
    def _():
        tail = h[tt - BLOCK:]
        for c in range(2):
            c2_ref[0, c] = kv_t(h, 1, c)
            c1_ref[0, c] = kv_t(tail, 0, c)
            cb_ref[0, c] = kv_t(tail, 3, c)


def _prompt_cache(x, ng, w_qkv, bd, gains):
    nb, seq, _ = x.shape
    tt = CACHE_TILE
    assert A_WINDOWS == (BLOCK, tt, 4 * tt) and B_WINDOW == BLOCK
    n_tiles = A_WINDOWS[2] // tt
    first = seq // tt - n_tiles
    out_shape = [jax.ShapeDtypeStruct((nb, 2, A_WIDTH, w), F32) for w in A_WINDOWS]
    out_shape.append(jax.ShapeDtypeStruct((nb, 2, B_KV_WIDTH, B_WINDOW), F32))
    out_specs = [pl.BlockSpec((1, 2, A_WIDTH, BLOCK), lambda n, j: (n, 0, 0, 0)),
                 pl.BlockSpec((1, 2, A_WIDTH, tt), lambda n, j: (n, 0, 0, 0)),
                 pl.BlockSpec((1, 2, A_WIDTH, tt), lambda n, j: (n, 0, 0, j)),
                 pl.BlockSpec((1, 2, B_KV_WIDTH, BLOCK), lambda n, j: (n, 0, 0, 0))]
    return pl.pallas_call(
        functools.partial(_prompt_cache_body, tt=tt),
        grid=(nb, n_tiles),
        in_specs=[pl.BlockSpec((1, tt, D_MODEL), lambda n, j: (n, first + j, 0)),
                  _resident(ng.shape), _resident((D_MODEL, QKV_WIDTH)), _resident(bd.shape),
                  _resident(gains.shape)],
        out_specs=out_specs,
        out_shape=out_shape,
        scratch_shapes=[pltpu.VMEM((tt, A_WIDTH), F32)],
        compiler_params=_params(dimension_semantics=("arbitrary", "arbitrary")),
        name="prompt_cache",
    )(x, ng, w_qkv, bd, gains)


def _stat_lane(pos):
    return pos if pos % 2 else HEAD_DIM + pos


def _softmax_block(s, bias, v_ext):
    s = s + bias
    m = jnp.max(s, axis=-1, keepdims=True)
    p = jnp.exp2(s - m).astype(BF16)
    return m, jnp.dot(p, v_ext, preferred_element_type=F32)


def _attn_body(q_ref, kp_ref, kc_ref, vp_ref, vc_ref, bias_ref, o_ref, ml_ref,
               *, shared_kv, head_of_pos, per_seq):
    q_rows = q_ref.shape[1]
    n_blocks = q_rows // BLOCK
    first = jnp.minimum(pl.program_id(0) % per_seq, 1)
    lane = lax.broadcasted_iota(jnp.int32, (BLOCK, LANES), 1)
    lo = lane < HEAD_DIM
    lo_kv = lax.broadcasted_iota(jnp.int32, (BLOCK + q_rows, LANES), 1) < HEAD_DIM
    one = jnp.ones((), BF16)
    ml_ref[0, :, :LANES] = jnp.zeros((q_rows, LANES), F32)
    ml_ref[0, :, LANES:] = jnp.ones((q_rows, LANES), F32)

    def load_kv(cols):
        k_all = jnp.concatenate([kp_ref[0, :, cols], kc_ref[0, :, cols]], axis=0)
        v_all = jnp.concatenate([vp_ref[0, :, cols], vc_ref[0, :, cols]], axis=0)
        return k_all, (jnp.where(lo_kv, v_all, one), jnp.where(lo_kv, one, v_all))

    if shared_kv:
        k_all, v_ext = load_kv(slice(0, LANES))
    for i in range(N_HEADS // 2):
        cols = slice(i * LANES, (i + 1) * LANES)
        if not shared_kv:
            k_all, v_ext = load_kv(cols)
        for b in range(n_blocks):
            sel = first if b == 0 else 1
            keys = slice(b * BLOCK, (b + 2) * BLOCK)
            rows = slice(b * BLOCK, (b + 1) * BLOCK)
            qp = q_ref[0, rows, cols]
            zero = jnp.zeros_like(qp)
            q2 = jnp.concatenate([jnp.where(lo, qp, zero), jnp.where(lo, zero, qp)], axis=0)
            s2 = lax.dot_general(q2, k_all[keys], (((1,), (1,)), ((), ())), preferred_element_type=F32)
            m_pair, pv_pair = [], []
            for hh in range(2):
                m, pv = _softmax_block(s2[hh * BLOCK:(hh + 1) * BLOCK],
                                       bias_ref[sel, head_of_pos[2 * i + hh]], v_ext[hh][keys])
                m_pair.append(m)
                pv_pair.append(pv)
            o_ref[0, rows, cols] = jnp.where(lo, pv_pair[0], pv_pair[1])
            for hh in range(2):
                at = _stat_lane(2 * i + hh)
                ml_ref[0, rows, at:at + 1] = m_pair[hh]
                ml_ref[0, rows, LANES + at:LANES + at + 1] = pv_pair[hh][:, at:at + 1]
            yield


def _attention_job(q, k, v, bias, head_of_pos):
    ns, steps, _ = q.shape
    kv_width = k.shape[-1]
    q_rows = min(Q_ROWS, steps)
    ratio = q_rows // BLOCK
    per_seq = steps // q_rows
    here = lambda i: (i // per_seq, i % per_seq, 0)
    qspec = pl.BlockSpec((1, q_rows, A_WIDTH), here)
    cur = pl.BlockSpec((1, q_rows, kv_width), here)
    prev = pl.BlockSpec((1, BLOCK, kv_width),
                        lambda i: (i // per_seq, jnp.maximum((i % per_seq) * ratio - 1, 0), 0))
    return _Job(
        body=functools.partial(_attn_body, shared_kv=kv_width == LANES, head_of_pos=head_of_pos,
                               per_seq=per_seq),
        steps=ns * per_seq,
        in_specs=[qspec, prev, cur, prev, cur, _resident(bias.shape)],
        out_specs=[qspec, pl.BlockSpec((1, q_rows, 2 * LANES), here)],
        out_shape=[jax.ShapeDtypeStruct((ns, steps, A_WIDTH), F32),
                   jax.ShapeDtypeStruct((ns, steps, 2 * LANES), F32)],
        scratch_shapes=[],
        args=(q, k, k, v, v, bias),
        name="attention",
        phases=(N_HEADS // 2) * ratio)


def _head_expand(w, e_ref):
    return jnp.dot(w.astype(BF16), e_ref[...], preferred_element_type=F32)


MERGE_PHASES = 8


def _merge_math(store, x, ng_ref, wg_ref, pa_ref, pb_ref, wo_ref, e_ref, sink_ref, o_groups, ml_groups, o_b,
                ml_b):
    ms = [ml[:, :LANES] for ml in ml_groups]
    ls = [ml[:, LANES:] for ml in ml_groups]
    big = jnp.maximum(jnp.maximum(ms[0], ms[1]), ms[2])
    es = [jnp.exp2(m - big) for m in ms]
    inv = 1.0 / (es[0] * ls[0] + es[1] * ls[1] + es[2] * ls[2])
    o_a = None
    for e, o in zip(es, o_groups):
        term = _head_expand(e * inv, e_ref) * o
        o_a = term if o_a is None else o_a + term
    inv_b = 1.0 / (ml_b[:, LANES:] + jnp.exp2(sink_ref[...] - ml_b[:, :LANES]))
    o_b = _head_expand(inv_b, e_ref) * o_b
    yield
    h = _rms_rows(x, ng_ref[...]).astype(BF16)
    gate = []
    for c0, c1 in ((0, A_WIDTH), (A_WIDTH, 2 * A_WIDTH), (2 * A_WIDTH, 2 * A_WIDTH + D_MODEL),
                   (2 * A_WIDTH + D_MODEL, 2 * A_WIDTH + 2 * D_MODEL)):
        gate.append(jnp.dot(h, wg_ref[:, c0:c1], preferred_element_type=F32))
        yield
    z_a, z_b, g_a, g_b = gate
    a = (o_a * (z_a * _sigmoid(z_a))).astype(BF16)
    b = (o_b * (z_b * _sigmoid(z_b))).astype(BF16)
    mixed = _sigmoid(g_a) * jnp.dot(a, pa_ref[...], preferred_element_type=F32)
    yield
    mixed = mixed + _sigmoid(g_b) * jnp.dot(b, pb_ref[...], preferred_element_type=F32)
    yield
    store(x + jnp.dot(mixed.astype(BF16), wo_ref[...], preferred_element_type=F32))
    yield


def _prompt_merge_body(x_ref, ng_ref, wg_ref, pa_ref, pb_ref, wo_ref, e_ref, sink_ref,
                       o1_ref, l1_ref, o2_ref, l2_ref, o3_ref, l3_ref, ob_ref, lb_ref, y_ref,
                       o_scr, l_scr, *, tt):
    for gi, (o_ref, l_ref, d) in enumerate(((o2_ref, l2_ref, 4), (o3_ref, l3_ref, 16))):
        rows = tt // d
        for r in range(d):
            for j in range(2):
                l_scr[gi, j, pl.ds(r, rows, stride=d), :] = l_ref[0, r, :, j * LANES:(j + 1) * LANES]
            for j in range(A_WIDTH // LANES):
                o_scr[gi, j, pl.ds(r, rows, stride=d), :] = o_ref[0, r, :, j * LANES:(j + 1) * LANES]
        yield
    o_groups = [o1_ref[0, 0]] + [
        jnp.concatenate([o_scr[gi, j] for j in range(A_WIDTH // LANES)], axis=1) for gi in range(2)]
    ml_groups = [l1_ref[0, 0]] + [jnp.concatenate([l_scr[gi, 0], l_scr[gi, 1]], axis=1) for gi in range(2)]

    def store(y):
        y_ref[0] = y

    yield from _merge_math(store, x_ref[0], ng_ref, wg_ref, pa_ref, pb_ref, wo_ref, e_ref, sink_ref,
                           o_groups, ml_groups, ob_ref[0], lb_ref[0])


def _prompt_merge_job(x, ng, w_gates, p_a, p_b, w_o, expand, sink_row, o_list, ml_list, o_b, ml_b):
    nb, seq, _ = x.shape
    tt = TOKEN_TILE
    tiles = seq // tt
    tile3 = lambda i: (i // tiles, i % tiles, 0)
    tile4 = lambda i: (i // tiles, 0, i % tiles, 0)
    in_specs = [pl.BlockSpec((1, tt, D_MODEL), tile3)]
    in_specs += [_resident(a.shape) for a in (ng, w_gates, p_a, p_b, w_o, expand, sink_row)]
    args = [x, ng, w_gates, p_a, p_b, w_o, expand, sink_row]
    for g in range(3):
        d = A_DILATIONS[g]
        in_specs.append(pl.BlockSpec((1, d, tt // d, A_WIDTH), tile4))
        in_specs.append(pl.BlockSpec((1, d, tt // d, 2 * LANES), tile4))
        args += [o_list[g].reshape(nb, d, seq // d, A_WIDTH), ml_list[g].reshape(nb, d, seq // d, 2 * LANES)]
    in_specs.append(pl.BlockSpec((1, tt, A_WIDTH), tile3))
    in_specs.append(pl.BlockSpec((1, tt, 2 * LANES), tile3))
    args += [o_b, ml_b]
    return _Job(
        body=functools.partial(_prompt_merge_body, tt=tt),
        steps=nb * tiles,
        in_specs=in_specs,
        out_specs=[pl.BlockSpec((1, tt, D_MODEL), tile3)],
        out_shape=[jax.ShapeDtypeStruct((nb, seq, D_MODEL), F32)],
        scratch_shapes=[pltpu.VMEM((2, A_WIDTH // LANES, tt, LANES), F32),
                        pltpu.VMEM((2, 2, tt, LANES), F32)],
        args=args,
        name="prompt_merge",
        phases=MERGE_PHASES + 2)


def _sample_proj_body(x_ref, ng_ref, w_ref, bd_ref, gain_ref, q_ref, kva_ref, kvb_ref, kvta_ref, kvtb_ref,
                      y_scr):
    dec = x_ref.shape[0]
    h = _rms_rows(x_ref[...], ng_ref[...]).astype(BF16)
    kvta_ref[...] = jnp.zeros_like(kvta_ref)
    kvtb_ref[...] = jnp.zeros_like(kvtb_ref)
    for ci, chunk in enumerate(_CHUNKS):
        wd = chunk[1]
        y = _proj_chunk(h, w_ref, bd_ref, gain_ref, chunk)
        mixer, role = divmod(ci, 3)
        if role == 0:
            q_ref[:, mixer * A_WIDTH:(mixer + 1) * A_WIDTH] = y
            continue
        y_scr[:, :wd] = y
        y_t = y_scr[:, :wd].T
        if mixer < 3:
            kva_ref[mixer, role - 1] = y
            kvta_ref[mixer, role - 1, :, :dec] = y_t
        else:
            kvb_ref[role - 1] = y
            kvtb_ref[role - 1, :, :dec] = y_t


def _sample_proj(x, ng, w_qkv, bd, gains):
    dec = x.shape[0]
    assert dec <= LANES
    out_shape = [jax.ShapeDtypeStruct((dec, 4 * A_WIDTH), F32),
                 jax.ShapeDtypeStruct((3, 2, dec, A_WIDTH), F32),
                 jax.ShapeDtypeStruct((2, dec, B_KV_WIDTH), F32),
                 jax.ShapeDtypeStruct((3, 2, A_WIDTH, LANES), F32),
                 jax.ShapeDtypeStruct((2, B_KV_WIDTH, LANES), F32)]
    args = (x, ng, w_qkv, bd, gains)
    return pl.pallas_call(
        _sample_proj_body,
        grid=(1,),
        in_specs=[_whole(x.shape), _whole(ng.shape), _whole((D_MODEL, QKV_WIDTH)), _whole(bd.shape),
                  _whole(gains.shape)],
        out_specs=[_whole(o.shape) for o in out_shape],
        out_shape=out_shape,
        scratch_shapes=[pltpu.VMEM((dec, A_WIDTH), F32)],
        compiler_params=_params(dimension_semantics=("arbitrary",)),
        name="sample_proj",
    )(*args)


def _cache_stream_body(c_ref, q_ref, kn_ref, vn_ref, kvt_ref, bias_ref, out_ref, o_ref, ml_ref, *, nb, tiles):
    window = c_ref.shape[-1]
    reps = N_HEADS // tiles
    step = pl.program_id(0)
    head_lane = lax.broadcasted_iota(jnp.int32, (N_HEADS, A_WIDTH), 1) // HEAD_DIM
    head_row = lax.broadcasted_iota(jnp.int32, (N_HEADS, A_WIDTH), 0)
    own = head_lane == head_row
    spos = lax.broadcasted_iota(jnp.int32, (N_HEADS, LANES), 0)
    slane = lax.broadcasted_iota(jnp.int32, (N_HEADS, LANES), 1)
    stat = slane == jnp.where(spos % 2 == 1, spos, spos + HEAD_DIM)
    used = jnp.sum(stat.astype(F32), axis=0, keepdims=True) > 0.0
    lane_w = lax.broadcasted_iota(jnp.int32, (HEAD_DIM, window), 1)

    def stacked(tiles_):
        return jnp.concatenate([tiles_[pos % tiles] for pos in range(N_HEADS)], axis=0)

    def tiled(row):
        return row if reps == 1 else jnp.concatenate([row] * reps, axis=1)

    for bi in range(nb):
        n = step * nb + bi
        q_row = q_ref[n]
        kn_row = tiled(kn_ref[n])
        vn_row = tiled(vn_ref[n])
        q_bd = jnp.where(own, q_row, 0.0)
        k_all = stacked([c_ref[bi, 0, g].astype(BF16) for g in range(tiles)])
        v_all = stacked([c_ref[bi, 1, g].astype(BF16) for g in range(tiles)])
        s = jnp.dot(q_bd.astype(BF16), k_all, preferred_element_type=F32) + bias_ref[...]
        s_new = jnp.sum(q_bd * kn_row, axis=1, keepdims=True)
        m = jnp.maximum(jnp.max(s, axis=1, keepdims=True), s_new)
        p = jnp.exp2(s - m)
        p_new = jnp.exp2(s_new - m)
        l = jnp.sum(p, axis=1, keepdims=True) + p_new
        o_t = lax.dot_general(p.astype(BF16), v_all, (((1,), (1,)), ((), ())), preferred_element_type=F32)
        o_ref[n] = jnp.sum(jnp.where(own, o_t + p_new * vn_row, 0.0), axis=0, keepdims=True)
        ml_ref[n, :, :LANES] = jnp.sum(jnp.where(stat, m, 0.0), axis=0, keepdims=True)
        ml_ref[n, :, LANES:] = jnp.where(used, jnp.sum(jnp.where(stat, l, 0.0), axis=0, keepdims=True), 1.0)
        yield
        for c in range(2):
            for g in range(tiles):
                new = pltpu.roll(kvt_ref[c, g * HEAD_DIM:(g + 1) * HEAD_DIM, :], LANES - 1 - n, axis=1)
                if window > LANES:
                    new = jnp.concatenate([new] * (window // LANES), axis=1)
                shifted = pltpu.roll(c_ref[bi, c, g], window - 1, axis=1)
                out_ref[bi, c, g] = jnp.where(lane_w == window - 1, new, shifted)
                yield


def _cache_stream_job(cache_t, q, mixer, kv_new, kv_new_t, bias_rows, nb):
    dec, _, kv_heads, _, window = cache_t.shape
    nb = min(nb, dec)
    blk = (nb, 2, kv_heads, HEAD_DIM, window)
    here = lambda i: (i, 0, 0, 0, 0)
    kv_width = kv_heads * HEAD_DIM
    return _Job(
        body=functools.partial(_cache_stream_body, nb=nb, tiles=kv_heads),
        steps=dec // nb,
        in_specs=[pl.BlockSpec(blk, here),
                  pl.BlockSpec((dec, 1, A_WIDTH), lambda i: (0, 0, mixer)),
                  pl.BlockSpec((None, dec, 1, kv_width), lambda i: (0, 0, 0, 0)),
                  pl.BlockSpec((None, dec, 1, kv_width), lambda i: (1, 0, 0, 0)),
                  _resident(kv_new_t.shape), _resident(bias_rows.shape)],
        out_specs=[pl.BlockSpec(blk, here),
                   pl.BlockSpec((dec, 1, A_WIDTH), lambda i: (0, 0, 0)),
                   pl.BlockSpec((dec, 1, 2 * LANES), lambda i: (0, 0, 0))],
        out_shape=[jax.ShapeDtypeStruct(cache_t.shape, F32),
                   jax.ShapeDtypeStruct((dec, 1, A_WIDTH), F32),
                   jax.ShapeDtypeStruct((dec, 1, 2 * LANES), F32)],
        scratch_shapes=[],
        args=(cache_t, q, kv_new, kv_new, kv_new_t, bias_rows),
        name="cache_stream",
        phases=nb * (1 + 2 * kv_heads))


def _sample_merge_body(x_ref, ng_ref, wg_ref, pa_ref, pb_ref, wo_ref, e_ref, sink_ref,
                       o1_ref, l1_ref, o2_ref, l2_ref, o3_ref, l3_ref, ob_ref, lb_ref, y_ref):
    def store(y):
        y_ref[...] = y

    for _ in _merge_math(store, x_ref[...], ng_ref, wg_ref, pa_ref, pb_ref, wo_ref, e_ref, sink_ref,
                         [o1_ref[...], o2_ref[...], o3_ref[...]],
                         [l1_ref[...], l2_ref[...], l3_ref[...]], ob_ref[...], lb_ref[...]):
        pass


def _sample_merge(x, ng, w_gates, p_a, p_b, w_o, expand, sink_row, o_list, ml_list):
    args = [x, ng, w_gates, p_a, p_b, w_o, expand, sink_row]
    for o, l in zip(o_list, ml_list):
        args += [o, l]
    return pl.pallas_call(
        _sample_merge_body,
        grid=(1,),
        in_specs=[_whole(a.shape) for a in args],
        out_specs=_whole(x.shape),
        out_shape=jax.ShapeDtypeStruct(x.shape, F32),
        compiler_params=_params(dimension_semantics=("arbitrary",)),
        name="sample_merge",
    )(*args)


def _buffer_bias_rows(window, dilation):
    r = np.arange(window)
    steps = (window - r) // dilation
    slopes = np.exp2(-np.arange(1, N_HEADS + 1, dtype=np.float32))
    bias = -slopes[:, None] * steps[None].astype(np.float32)
    return np.where((r % dilation == 0)[None], bias * LOG2E, np.float32(NEG)).astype(np.float32)


def _band_bias_table():
    q = np.arange(BLOCK)[:, None]
    kk = np.arange(2 * BLOCK)[None, :]
    delta = q + BLOCK - kk
    valid = (delta >= 0) & (delta <= BLOCK)
    slopes = np.exp2(-np.arange(1, N_HEADS + 1, dtype=np.float32))
    bias = -slopes[:, None, None] * delta[None].astype(np.float32)
    normal = np.where(valid[None], bias, np.float32(NEG))
    first = np.where((valid & (kk >= BLOCK))[None], bias, np.float32(NEG))
    return (np.stack([first, normal]) * LOG2E).astype(np.float32)


def _block_diag_ones():
    i = np.arange(MXU_WIDTH)
    return (i[:, None] // HEAD_DIM == i[None, :] // HEAD_DIM).astype(np.float32)


def _head_expand_matrix():
    e = np.zeros((LANES, A_WIDTH), np.float32)
    for pos in range(N_HEADS):
        e[_stat_lane(pos), pos * HEAD_DIM:(pos + 1) * HEAD_DIM] = 1.0
    return e


def _b_column_order():
    return np.concatenate([np.arange(HEAD_DIM) + HEAD_DIM * hd for hd in B_HEAD_ORDER])


def _prepare_weights(norm_gain, w_in, qk_norm_a, qk_norm_b, b_sinks, w_branch_a, w_branch_b, w_out):
    w = w_in[0].astype(BF16)
    order = _b_column_order()

    def reorder_heads(cols):
        half = N_HEADS // 2
        return cols.reshape(-1, 2, half, HEAD_DIM).swapaxes(1, 2).reshape(cols.shape)

    assert B_HEAD_ORDER == tuple(h for pair in zip(range(4), range(4, 8)) for h in pair)
    b_q0, z_b0 = 9 * A_WIDTH, QKV_WIDTH + A_WIDTH
    w = lax.dynamic_update_slice(w, reorder_heads(w[:, b_q0:b_q0 + A_WIDTH]), (0, b_q0))
    w = lax.dynamic_update_slice(w, reorder_heads(w[:, z_b0:z_b0 + A_WIDTH]), (0, z_b0))
    w_qkv, w_gates = w, w[:, QKV_WIDTH:]
    scale = HEAD_DIM ** -0.5 * LOG2E
    rows = []
    for g in range(3):
        rows.append(jnp.tile(qk_norm_a[0, g, 0], N_HEADS) * scale)
        rows.append(jnp.tile(qk_norm_a[0, g, 1], N_HEADS))
    rows.append(jnp.tile(qk_norm_b[0, 0], N_HEADS) * scale)
    rows.append(jnp.tile(qk_norm_b[0, 1], N_HEADS))
    gains = jnp.stack(rows).astype(F32)
    lanes = np.array([_stat_lane(pos) for pos in range(N_HEADS)])
    sink_row = jnp.zeros((1, LANES), F32).at[0, lanes].set(b_sinks[0][np.array(B_HEAD_ORDER)].astype(F32) * LOG2E)
    return dict(ng=norm_gain.astype(F32), w_qkv=w_qkv, w_gates=w_gates, gains=gains, sink_row=sink_row,
                p_a=w_branch_a[0].astype(BF16), p_b=w_branch_b[0][order].astype(BF16),
                w_o=w_out[0].astype(BF16))


def _cache_out(t):
    nb, two, width, rows = t.shape
    t = t.reshape(nb, two, width // HEAD_DIM, HEAD_DIM, rows)
    return jnp.transpose(t, (0, 4, 1, 2, 3))[None]


STREAM_BATCH = (2, 1, 1, 8)


def _layer(x_prompt, x_sample, caches, wts, consts):
    nb, seq, _ = x_prompt.shape
    dec = x_sample.shape[0]
    x2 = x_sample.reshape(dec, D_MODEL)
    q_s, kv_a, kv_b, kvt_a, kvt_b = _sample_proj(x2, wts["ng"], wts["w_qkv"], consts["bd"], wts["gains"])

    def stream_job(mixer):
        cache_t = jnp.transpose(caches[mixer][0], (0, 2, 3, 4, 1))
        kv_new, kv_new_t = (kv_b, kvt_b) if mixer == 3 else (kv_a[mixer], kvt_a[mixer])
        return _cache_stream_job(cache_t, q_s[:, None, :], mixer, kv_new[:, :, None, :], kv_new_t,
                                 consts["rows"][mixer], STREAM_BATCH[mixer])

    def attention_job(g):
        d = A_DILATIONS[g]
        shape = (nb * d, seq // d, A_WIDTH)
        return _attention_job(*(a.reshape(shape) for a in qkv[3 * g:3 * g + 3]), consts["bias"], NATURAL_ORDER)

    qkv, a3 = _run_pair(
        _prompt_proj_job(x_prompt, wts["ng"], wts["w_qkv"], consts["bd"], wts["gains"]), stream_job(2),
        interleave=False)
    prompt_caches = [_cache_out(c) for c in
                     _prompt_cache(x_prompt, wts["ng"], wts["w_qkv"], consts["bd"], wts["gains"])]
    attn = [None] * 3
    attn[0] = _run(attention_job(0))
    attn[1] = _run(attention_job(1))
    attn[2], a1 = _run_pair(attention_job(2), stream_job(0), interleave=False)
    (o_b, ml_b), b = _run_pair(
        _attention_job(qkv[9], qkv[10], qkv[11], consts["bias"], B_HEAD_ORDER), stream_job(3),
        interleave=False)
    o_list, ml_list = [a[0] for a in attn], [a[1] for a in attn]
    (y_prompt,), a2 = _run_pair(
        _prompt_merge_job(x_prompt, wts["ng"], wts["w_gates"], wts["p_a"], wts["p_b"], wts["w_o"],
                          consts["expand"], wts["sink_row"], o_list, ml_list, o_b, ml_b), stream_job(1),
        interleave=True)

    streams = (a1, a2, a3, b)
    rows = lambda a: a.reshape(a.shape[0], a.shape[2])
    sample_caches = [jnp.transpose(s[0], (0, 4, 1, 2, 3))[None] for s in streams]
    y_sample = _sample_merge(x2, wts["ng"], wts["w_gates"], wts["p_a"], wts["p_b"], wts["w_o"],
                             consts["expand"], wts["sink_row"],
                             [rows(s[1]) for s in streams], [rows(s[2]) for s in streams])
    return y_prompt, y_sample.reshape(dec, 1, D_MODEL), prompt_caches, sample_caches


def _constants():
    assert all(B_HEAD_ORDER[pos] // (N_HEADS // B_KV_HEADS) == pos % B_KV_HEADS for pos in range(N_HEADS))
    rows = [jnp.asarray(_buffer_bias_rows(w, d)) for w, d in zip(A_WINDOWS, A_DILATIONS)]
    rows.append(jnp.asarray(_buffer_bias_rows(B_WINDOW, 1)[np.array(B_HEAD_ORDER)]))
    return dict(bias=jnp.asarray(_band_bias_table()), bd=jnp.asarray(_block_diag_ones(), BF16),
                expand=jnp.asarray(_head_expand_matrix(), BF16), rows=rows)


def kernel(x_prompt, x_sample, cache_a1_kv, cache_a2_kv, cache_a3_kv, cache_b_kv, norm_gain, w_in,
           qk_norm_a, qk_norm_b, b_sinks, w_branch_a, w_branch_b, w_out):
    wts = _prepare_weights(norm_gain, w_in, qk_norm_a, qk_norm_b, b_sinks, w_branch_a, w_branch_b, w_out)
    consts = _constants()
    y_prompt, y_sample, pc, sc = _layer(x_prompt, x_sample, (cache_a1_kv, cache_a2_kv, cache_a3_kv, cache_b_kv),
                                        wts, consts)
    return (y_prompt, y_sample, pc[0], pc[1], pc[2], pc[3], sc[0], sc[1], sc[2], sc[3])
```

```python
import functools
from typing import Callable, NamedTuple, Optional, Sequence

import numpy as np
import jax
import jax.numpy as jnp
from jax import lax
from jax.experimental import pallas as pl
from jax.experimental.pallas import tpu as pltpu

F32 = jnp.float32
BF16 = jnp.bfloat16

D_MODEL = 1024
HEAD_DIM = 64
N_HEADS = 8
A_WINDOWS = (128, 512, 2048)
A_DILATIONS = (1, 4, 16)
B_WINDOW = 128
B_KV_HEADS = 2
A_WIDTH = N_HEADS * HEAD_DIM
B_KV_WIDTH = B_KV_HEADS * HEAD_DIM
QKV_WIDTH = 9 * A_WIDTH + A_WIDTH + 2 * B_KV_WIDTH
BLOCK = 128
Q_ROWS = 2048
NORM_EPS = 1e-6
NEG = -1e30
LOG2E = float(np.log2(np.e))
LANES = 128
MXU_WIDTH = 256
TOKEN_TILE = 256
CACHE_TILE = 512
ROW_TILE = 256
VMEM_LIMIT = 56 * 1024 * 1024
B_HEAD_ORDER = (0, 4, 1, 5, 2, 6, 3, 7)
NATURAL_ORDER = tuple(range(N_HEADS))

_CHUNKS = []
for _g in range(3):
    _CHUNKS += [(1536 * _g, 512, 2 * _g), (1536 * _g + 512, 512, 2 * _g + 1), (1536 * _g + 1024, 512, None)]
_CHUNKS += [(4608, 512, 6), (5120, 128, 7), (5248, 128, None)]


def _params(**kw):
    return pltpu.CompilerParams(vmem_limit_bytes=VMEM_LIMIT, **kw)


def _resident(shape):
    nd = len(shape)
    return pl.BlockSpec(shape, lambda *_: (0,) * nd, pipeline_mode=pl.Buffered(1))


def _whole(shape):
    nd = len(shape)
    return pl.BlockSpec(shape, lambda *_: (0,) * nd)


class _Job(NamedTuple):
    body: Callable
    steps: int
    in_specs: Sequence
    out_specs: Sequence
    out_shape: Sequence
    scratch_shapes: Sequence
    args: Sequence
    name: str
    phases: int = 1


def _run(job):
    def body(*refs):
        for _ in job.body(*refs):
            pass

    return pl.pallas_call(
        body, grid=(job.steps,), in_specs=list(job.in_specs), out_specs=list(job.out_specs),
        out_shape=list(job.out_shape), scratch_shapes=list(job.scratch_shapes),
        compiler_params=_params(dimension_semantics=("arbitrary",)), name=job.name)(*job.args)


def _run_pair(a, b, interleave):
    if a.steps != b.steps:
        return _run(a), _run(b)
    n_in = (len(a.in_specs), len(b.in_specs))
    n_out = (len(a.out_specs), len(b.out_specs))
    n_scr = len(a.scratch_shapes)

    def body(*refs):
        ins, rest = refs[:sum(n_in)], refs[sum(n_in):]
        outs, scr = rest[:sum(n_out)], rest[sum(n_out):]
        gens = [a.body(*ins[:n_in[0]], *outs[:n_out[0]], *scr[:n_scr]),
                b.body(*ins[n_in[0]:], *outs[n_out[0]:], *scr[n_scr:])]
        credit = [0, 0]
        live = [True, True]
        while any(live):
            ahead = credit[0] * b.phases > credit[1] * a.phases if interleave else False
            k = 0 if (live[0] and not (live[1] and ahead)) else 1
            try:
                next(gens[k])
                credit[k] += 1
            except StopIteration:
                live[k] = False

    outs = pl.pallas_call(
        body, grid=(a.steps,), in_specs=list(a.in_specs) + list(b.in_specs),
        out_specs=list(a.out_specs) + list(b.out_specs), out_shape=list(a.out_shape) + list(b.out_shape),
        scratch_shapes=list(a.scratch_shapes) + list(b.scratch_shapes),
        compiler_params=_params(dimension_semantics=("arbitrary",)), name=a.name + "_" + b.name,
    )(*a.args, *b.args)
    return outs[:n_out[0]], outs[n_out[0]:]


def _rms_rows(x, gain_row):
    ms = jnp.mean(x * x, axis=-1, keepdims=True)
    return (x * lax.rsqrt(ms + NORM_EPS)) * gain_row


def _proj_chunk(h, w_ref, bd_ref, gain_ref, chunk):
    off, wd, gain_row = chunk
    parts = []
    for r0 in range(0, h.shape[0], ROW_TILE):
        acc = jnp.dot(h[r0:r0 + ROW_TILE], w_ref[:, off:off + wd], preferred_element_type=F32)
        if gain_row is not None:
            sq = (acc * acc).astype(BF16)
            step = min(wd, MXU_WIDTH)
            ss = jnp.concatenate(
                [jnp.dot(sq[:, c0:c0 + step], bd_ref[:step, :step], preferred_element_type=F32)
                 for c0 in range(0, wd, step)], axis=1)
            acc = (acc * lax.rsqrt(ss * (1.0 / HEAD_DIM) + NORM_EPS)) * gain_ref[gain_row:gain_row + 1, :wd]
        parts.append(acc)
    return parts[0] if len(parts) == 1 else jnp.concatenate(parts, axis=0)


def _sigmoid(z):
    return 1.0 / (1.0 + jnp.exp(-z))


def _prompt_proj_body(x_ref, ng_ref, w_ref, bd_ref, gain_ref, *refs, tt):
    outs = refs[:12]
    perm_ref = refs[12]
    h = _rms_rows(x_ref[0], ng_ref[...]).astype(BF16)
    for ci, chunk in enumerate(_CHUNKS):
        y = _proj_chunk(h, w_ref, bd_ref, gain_ref, chunk)
        mixer, role = divmod(ci, 3)
        o_ref = outs[ci]
        d = A_DILATIONS[mixer] if mixer < 3 else 1
        if mixer == 3:
            o_ref[0] = y.astype(BF16)
        elif d == 1:
            o_ref[0, 0] = y.astype(BF16)
        else:
            rows = tt // d
            for j in range(A_WIDTH // LANES):
                perm_ref[j] = y[:, j * LANES:(j + 1) * LANES]
            for r in range(d):
                for j in range(A_WIDTH // LANES):
                    o_ref[0, r, :, j * LANES:(j + 1) * LANES] = (
                        perm_ref[j, pl.ds(r, rows, stride=d), :].astype(BF16))
        yield


def _prompt_proj_job(x, ng, w_qkv, bd, gains):
    nb, seq, _ = x.shape
    tt = TOKEN_TILE
    tiles = seq // tt
    out_shape, out_specs = [], []
    for g in range(3):
        d = A_DILATIONS[g]
        for _ in range(3):
            out_shape.append(jax.ShapeDtypeStruct((nb, d, seq // d, A_WIDTH), BF16))
            out_specs.append(pl.BlockSpec((1, d, tt // d, A_WIDTH), lambda i: (i // tiles, 0, i % tiles, 0)))
    for wd in (A_WIDTH, B_KV_WIDTH, B_KV_WIDTH):
        out_shape.append(jax.ShapeDtypeStruct((nb, seq, wd), BF16))
        out_specs.append(pl.BlockSpec((1, tt, wd), lambda i: (i // tiles, i % tiles, 0)))
    return _Job(
        body=functools.partial(_prompt_proj_body, tt=tt),
        steps=nb * tiles,
        in_specs=[pl.BlockSpec((1, tt, D_MODEL), lambda i: (i // tiles, i % tiles, 0)),
                  _resident(ng.shape), _resident((D_MODEL, QKV_WIDTH)), _resident(bd.shape),
                  _resident(gains.shape)],
        out_specs=out_specs,
        out_shape=out_shape,
        scratch_shapes=[pltpu.VMEM((A_WIDTH // LANES, tt, LANES), F32)],
        args=(x, ng, w_qkv, bd, gains),
        name="prompt_proj",
        phases=len(_CHUNKS))


def _prompt_cache_body(x_ref, ng_ref, w_ref, bd_ref, gain_ref, c1_ref, c2_ref, c3_ref, cb_ref, y_scr, *, tt):
    j = pl.program_id(1)
    last = pl.num_programs(1) - 1
    h = _rms_rows(x_ref[0], ng_ref[...]).astype(BF16)

    def kv_t(rows, mixer, c):
        wd = _CHUNKS[3 * mixer + 1 + c][1]
        y_scr[:rows.shape[0], :wd] = _proj_chunk(rows, w_ref, bd_ref, gain_ref, _CHUNKS[3 * mixer + 1 + c])
        return y_scr[:rows.shape[0], :wd].T

    for c in range(2):
        c3_ref[0, c] = kv_t(h, 2, c)

    @pl.when(j == last)
    def _():
        tail = h[tt - BLOCK:]
        for c in range(2):
            c2_ref[0, c] = kv_t(h, 1, c)
            c1_ref[0, c] = kv_t(tail, 0, c)
            cb_ref[0, c] = kv_t(tail, 3, c)


def _prompt_cache(x, ng, w_qkv, bd, gains):
    nb, seq, _ = x.shape
    tt = CACHE_TILE
    assert A_WINDOWS == (BLOCK, tt, 4 * tt) and B_WINDOW == BLOCK
    n_tiles = A_WINDOWS[2] // tt
    first = seq // tt - n_tiles
    out_shape = [jax.ShapeDtypeStruct((nb, 2, A_WIDTH, w), F32) for w in A_WINDOWS]
    out_shape.append(jax.ShapeDtypeStruct((nb, 2, B_KV_WIDTH, B_WINDOW), F32))
    out_specs = [pl.BlockSpec((1, 2, A_WIDTH, BLOCK), lambda n, j: (n, 0, 0, 0)),
                 pl.BlockSpec((1, 2, A_WIDTH, tt), lambda n, j: (n, 0, 0, 0)),
                 pl.BlockSpec((1, 2, A_WIDTH, tt), lambda n, j: (n, 0, 0, j)),
                 pl.BlockSpec((1, 2, B_KV_WIDTH, BLOCK), lambda n, j: (n, 0, 0, 0))]
    return pl.pallas_call(
        functools.partial(_prompt_cache_body, tt=tt),
        grid=(nb, n_tiles),
        in_specs=[pl.BlockSpec((1, tt, D_MODEL), lambda n, j: (n, first + j, 0)),
                  _resident(ng.shape), _resident((D_MODEL, QKV_WIDTH)), _resident(bd.shape),
                  _resident(gains.shape)],
        out_specs=out_specs,
        out_shape=out_shape,
        scratch_shapes=[pltpu.VMEM((tt, A_WIDTH), F32)],
        compiler_params=_params(dimension_semantics=("arbitrary", "arbitrary")),
        name="prompt_cache",
    )(x, ng, w_qkv, bd, gains)


def _stat_lane(pos):
    return pos if pos % 2 else HEAD_DIM + pos


def _softmax_block(s, bias, v_ext):
    s = s + bias
    m = jnp.max(s, axis=-1, keepdims=True)
    p = jnp.exp2(s - m).astype(BF16)
    return m, jnp.dot(p, v_ext, preferred_element_type=F32)


def _attn_body(q_ref, kp_ref, kc_ref, vp_ref, vc_ref, bias_ref, o_ref, ml_ref,
               *, shared_kv, head_of_pos, per_seq):
    n_seq, q_rows = q_ref.shape[:2]
    n_blocks = q_rows // BLOCK
    first = jnp.minimum(pl.program_id(0) % per_seq, 1)
    lane = lax.broadcasted_iota(jnp.int32, (BLOCK, LANES), 1)
    lo = lane < HEAD_DIM
    lo_kv = lax.broadcasted_iota(jnp.int32, (BLOCK + q_rows, LANES), 1) < HEAD_DIM
    one = jnp.ones((), BF16)
    ml_ref[:, :, :LANES] = jnp.zeros((n_seq, q_rows, LANES), F32)
    ml_ref[:, :, LANES:] = jnp.ones((n_seq, q_rows, LANES), F32)

    def load_kv(si, cols):
        k_all = jnp.concatenate([kp_ref[si, :, cols], kc_ref[si, :, cols]], axis=0)
        v_all = jnp.concatenate([vp_ref[si, :, cols], vc_ref[si, :, cols]], axis=0)
        return k_all, (jnp.where(lo_kv, v_all, one), jnp.where(lo_kv, one, v_all))

    for si, i in ((si, i) for si in range(n_seq) for i in range(N_HEADS // 2)):
        cols = slice(i * LANES, (i + 1) * LANES)
        if i == 0 or not shared_kv:
            k_all, v_ext = load_kv(si, slice(0, LANES) if shared_kv else cols)
        for b in range(n_blocks):
            sel = first if b == 0 else 1
            keys = slice(b * BLOCK, (b + 2) * BLOCK)
            rows = slice(b * BLOCK, (b + 1) * BLOCK)
            qp = q_ref[si, rows, cols]
            zero = jnp.zeros_like(qp)
            q2 = jnp.concatenate([jnp.where(lo, qp, zero), jnp.where(lo, zero, qp)], axis=0)
            s2 = lax.dot_general(q2, k_all[keys], (((1,), (1,)), ((), ())), preferred_element_type=F32)
            m_pair, pv_pair = [], []
            for hh in range(2):
                m, pv = _softmax_block(s2[hh * BLOCK:(hh + 1) * BLOCK],
                                       bias_ref[sel, head_of_pos[2 * i + hh]], v_ext[hh][keys])
                m_pair.append(m)
                pv_pair.append(pv)
            o_ref[si, rows, cols] = jnp.where(lo, pv_pair[0], pv_pair[1])
            for hh in range(2):
                at = _stat_lane(2 * i + hh)
                ml_ref[si, rows, at:at + 1] = m_pair[hh]
                ml_ref[si, rows, LANES + at:LANES + at + 1] = pv_pair[hh][:, at:at + 1]
            yield


def _attention_job(q, k, v, bias, head_of_pos):
    ns, steps, _ = q.shape
    kv_width = k.shape[-1]
    q_rows = min(Q_ROWS, steps)
    n_seq = min(Q_ROWS // q_rows, ns)
    ratio = q_rows // BLOCK
    per_seq = steps // q_rows
    here = lambda i: (i // per_seq, i % per_seq, 0)
    qspec = pl.BlockSpec((n_seq, q_rows, A_WIDTH), here)
    cur = pl.BlockSpec((n_seq, q_rows, kv_width), here)
    prev = pl.BlockSpec((n_seq, BLOCK, kv_width),
                        lambda i: (i // per_seq, jnp.maximum((i % per_seq) * ratio - 1, 0), 0))
    return _Job(
        body=functools.partial(_attn_body, shared_kv=kv_width == LANES, head_of_pos=head_of_pos,
                               per_seq=per_seq),
        steps=(ns // n_seq) * per_seq,
        in_specs=[qspec, prev, cur, prev, cur, _resident(bias.shape)],
        out_specs=[qspec, pl.BlockSpec((n_seq, q_rows, 2 * LANES), here)],
        out_shape=[jax.ShapeDtypeStruct((ns, steps, A_WIDTH), F32),
                   jax.ShapeDtypeStruct((ns, steps, 2 * LANES), F32)],
        scratch_shapes=[],
        args=(q, k, k, v, v, bias),
        name="attention",
        phases=(N_HEADS // 2) * ratio * n_seq)


def _head_expand(w, e_ref):
    return jnp.dot(w.astype(BF16), e_ref[...], preferred_element_type=F32)


MERGE_PHASES = 8


def _merge_math(store, x, ng_ref, wg_ref, pa_ref, pb_ref, wo_ref, e_ref, sink_ref, o_groups, ml_groups, o_b,
                ml_b):
    ms = [ml[:, :LANES] for ml in ml_groups]
    ls = [ml[:, LANES:] for ml in ml_groups]
    big = jnp.maximum(jnp.maximum(ms[0], ms[1]), ms[2])
    es = [jnp.exp2(m - big) for m in ms]
    inv = 1.0 / (es[0] * ls[0] + es[1] * ls[1] + es[2] * ls[2])
    o_a = None
    for e, o in zip(es, o_groups):
        term = _head_expand(e * inv, e_ref) * o
        o_a = term if o_a is None else o_a + term
    inv_b = 1.0 / (ml_b[:, LANES:] + jnp.exp2(sink_ref[...] - ml_b[:, :LANES]))
    o_b = _head_expand(inv_b, e_ref) * o_b
    yield
    h = _rms_rows(x, ng_ref[...]).astype(BF16)
    gate = []
    for c0, c1 in ((0, A_WIDTH), (A_WIDTH, 2 * A_WIDTH), (2 * A_WIDTH, 2 * A_WIDTH + D_MODEL),
                   (2 * A_WIDTH + D_MODEL, 2 * A_WIDTH + 2 * D_MODEL)):
        gate.append(jnp.dot(h, wg_ref[:, c0:c1], preferred_element_type=F32))
        yield
    z_a, z_b, g_a, g_b = gate
    a = (o_a * (z_a * _sigmoid(z_a))).astype(BF16)
    b = (o_b * (z_b * _sigmoid(z_b))).astype(BF16)
    mixed = _sigmoid(g_a) * jnp.dot(a, pa_ref[...], preferred_element_type=F32)
    yield
    mixed = mixed + _sigmoid(g_b) * jnp.dot(b, pb_ref[...], preferred_element_type=F32)
    yield
    store(x + jnp.dot(mixed.astype(BF16), wo_ref[...], preferred_element_type=F32))
    yield


def _prompt_merge_body(x_ref, ng_ref, wg_ref, pa_ref, pb_ref, wo_ref, e_ref, sink_ref,
                       o1_ref, l1_ref, o2_ref, l2_ref, o3_ref, l3_ref, ob_ref, lb_ref, y_ref,
                       o_scr, l_scr, *, tt):
    for gi, (o_ref, l_ref, d) in enumerate(((o2_ref, l2_ref, 4), (o3_ref, l3_ref, 16))):
        rows = tt // d
        for r in range(d):
            for j in range(2):
                l_scr[gi, j, pl.ds(r, rows, stride=d), :] = l_ref[0, r, :, j * LANES:(j + 1) * LANES]
            for j in range(A_WIDTH // LANES):
                o_scr[gi, j, pl.ds(r, rows, stride=d), :] = o_ref[0, r, :, j * LANES:(j + 1) * LANES]
        yield
    o_groups = [o1_ref[0, 0]] + [
        jnp.concatenate([o_scr[gi, j] for j in range(A_WIDTH // LANES)], axis=1) for gi in range(2)]
    ml_groups = [l1_ref[0, 0]] + [jnp.concatenate([l_scr[gi, 0], l_scr[gi, 1]], axis=1) for gi in range(2)]

    def store(y):
        y_ref[0] = y

    yield from _merge_math(store, x_ref[0], ng_ref, wg_ref, pa_ref, pb_ref, wo_ref, e_ref, sink_ref,
                           o_groups, ml_groups, ob_ref[0], lb_ref[0])


def _prompt_merge_job(x, ng, w_gates, p_a, p_b, w_o, expand, sink_row, o_list, ml_list, o_b, ml_b):
    nb, seq, _ = x.shape
    tt = TOKEN_TILE
    tiles = seq // tt
    tile3 = lambda i: (i // tiles, i % tiles, 0)
    tile4 = lambda i: (i // tiles, 0, i % tiles, 0)
    in_specs = [pl.BlockSpec((1, tt, D_MODEL), tile3)]
    in_specs += [_resident(a.shape) for a in (ng, w_gates, p_a, p_b, w_o, expand, sink_row)]
    args = [x, ng, w_gates, p_a, p_b, w_o, expand, sink_row]
    for g in range(3):
        d = A_DILATIONS[g]
        in_specs.append(pl.BlockSpec((1, d, tt // d, A_WIDTH), tile4))
        in_specs.append(pl.BlockSpec((1, d, tt // d, 2 * LANES), tile4))
        args += [o_list[g].reshape(nb, d, seq // d, A_WIDTH), ml_list[g].reshape(nb, d, seq // d, 2 * LANES)]
    in_specs.append(pl.BlockSpec((1, tt, A_WIDTH), tile3))
    in_specs.append(pl.BlockSpec((1, tt, 2 * LANES), tile3))
    args += [o_b, ml_b]
    return _Job(
        body=functools.partial(_prompt_merge_body, tt=tt),
        steps=nb * tiles,
        in_specs=in_specs,
        out_specs=[pl.BlockSpec((1, tt, D_MODEL), tile3)],
        out_shape=[jax.ShapeDtypeStruct((nb, seq, D_MODEL), F32)],
        scratch_shapes=[pltpu.VMEM((2, A_WIDTH // LANES, tt, LANES), F32),
                        pltpu.VMEM((2, 2, tt, LANES), F32)],
        args=args,
        name="prompt_merge",
        phases=MERGE_PHASES + 2)


def _sample_proj_body(x_ref, ng_ref, w_ref, bd_ref, gain_ref, q_ref, kva_ref, kvb_ref, kvta_ref, kvtb_ref,
                      y_scr):
    dec = x_ref.shape[0]
    h = _rms_rows(x_ref[...], ng_ref[...]).astype(BF16)
    kvta_ref[...] = jnp.zeros_like(kvta_ref)
    kvtb_ref[...] = jnp.zeros_like(kvtb_ref)
    for ci, chunk in enumerate(_CHUNKS):
        wd = chunk[1]
        y = _proj_chunk(h, w_ref, bd_ref, gain_ref, chunk)
        mixer, role = divmod(ci, 3)
        if role == 0:
            q_ref[:, mixer * A_WIDTH:(mixer + 1) * A_WIDTH] = y
            continue
        y_scr[:, :wd] = y
        y_t = y_scr[:, :wd].T
        if mixer < 3:
            kva_ref[mixer, role - 1] = y
            kvta_ref[mixer, role - 1, :, :dec] = y_t
        else:
            kvb_ref[role - 1] = y
            kvtb_ref[role - 1, :, :dec] = y_t


def _sample_proj(x, ng, w_qkv, bd, gains):
    dec = x.shape[0]
    assert dec <= LANES
    out_shape = [jax.ShapeDtypeStruct((dec, 4 * A_WIDTH), F32),
                 jax.ShapeDtypeStruct((3, 2, dec, A_WIDTH), F32),
                 jax.ShapeDtypeStruct((2, dec, B_KV_WIDTH), F32),
                 jax.ShapeDtypeStruct((3, 2, A_WIDTH, LANES), F32),
                 jax.ShapeDtypeStruct((2, B_KV_WIDTH, LANES), F32)]
    args = (x, ng, w_qkv, bd, gains)
    return pl.pallas_call(
        _sample_proj_body,
        grid=(1,),
        in_specs=[_whole(x.shape), _whole(ng.shape), _whole((D_MODEL, QKV_WIDTH)), _whole(bd.shape),
                  _whole(gains.shape)],
        out_specs=[_whole(o.shape) for o in out_shape],
        out_shape=out_shape,
        scratch_shapes=[pltpu.VMEM((dec, A_WIDTH), F32)],
        compiler_params=_params(dimension_semantics=("arbitrary",)),
        name="sample_proj",
    )(*args)


def _cache_stream_body(c_ref, q_ref, kn_ref, vn_ref, kvt_ref, bias_ref, out_ref, o_ref, ml_ref, *, nb, tiles):
    window = c_ref.shape[-1]
    reps = N_HEADS // tiles
    step = pl.program_id(0)
    head_lane = lax.broadcasted_iota(jnp.int32, (N_HEADS, A_WIDTH), 1) // HEAD_DIM
    head_row = lax.broadcasted_iota(jnp.int32, (N_HEADS, A_WIDTH), 0)
    own = head_lane == head_row
    spos = lax.broadcasted_iota(jnp.int32, (N_HEADS, LANES), 0)
    slane = lax.broadcasted_iota(jnp.int32, (N_HEADS, LANES), 1)
    stat = slane == jnp.where(spos % 2 == 1, spos, spos + HEAD_DIM)
    used = jnp.sum(stat.astype(F32), axis=0, keepdims=True) > 0.0
    lane_w = lax.broadcasted_iota(jnp.int32, (HEAD_DIM, window), 1)

    def stacked(tiles_):
        return jnp.concatenate([tiles_[pos % tiles] for pos in range(N_HEADS)], axis=0)

    def tiled(row):
        return row if reps == 1 else jnp.concatenate([row] * reps, axis=1)

    for bi in range(nb):
        n = step * nb + bi
        q_row = q_ref[n]
        kn_row = tiled(kn_ref[n])
        vn_row = tiled(vn_ref[n])
        q_bd = jnp.where(own, q_row, 0.0)
        k_all = stacked([c_ref[bi, 0, g].astype(BF16) for g in range(tiles)])
        v_all = stacked([c_ref[bi, 1, g].astype(BF16) for g in range(tiles)])
        s = jnp.dot(q_bd.astype(BF16), k_all, preferred_element_type=F32) + bias_ref[...]
        s_new = jnp.sum(q_bd * kn_row, axis=1, keepdims=True)
        m = jnp.maximum(jnp.max(s, axis=1, keepdims=True), s_new)
        p = jnp.exp2(s - m)
        p_new = jnp.exp2(s_new - m)
        l = jnp.sum(p, axis=1, keepdims=True) + p_new
        o_t = lax.dot_general(p.astype(BF16), v_all, (((1,), (1,)), ((), ())), preferred_element_type=F32)
        o_ref[n] = jnp.sum(jnp.where(own, o_t + p_new * vn_row, 0.0), axis=0, keepdims=True)
        ml_ref[n, :, :LANES] = jnp.sum(jnp.where(stat, m, 0.0), axis=0, keepdims=True)
        ml_ref[n, :, LANES:] = jnp.where(used, jnp.sum(jnp.where(stat, l, 0.0), axis=0, keepdims=True), 1.0)
        yield
        for c in range(2):
            for g in range(tiles):
                new = pltpu.roll(kvt_ref[c, g * HEAD_DIM:(g + 1) * HEAD_DIM, :], LANES - 1 - n, axis=1)
                if window > LANES:
                    new = jnp.concatenate([new] * (window // LANES), axis=1)
                shifted = pltpu.roll(c_ref[bi, c, g], window - 1, axis=1)
                out_ref[bi, c, g] = jnp.where(lane_w == window - 1, new, shifted)
                yield


def _cache_stream_job(cache_t, q, mixer, kv_new, kv_new_t, bias_rows, nb):
    dec, _, kv_heads, _, window = cache_t.shape
    nb = min(nb, dec)
    blk = (nb, 2, kv_heads, HEAD_DIM, window)
    here = lambda i: (i, 0, 0, 0, 0)
    kv_width = kv_heads * HEAD_DIM
    return _Job(
        body=functools.partial(_cache_stream_body, nb=nb, tiles=kv_heads),
        steps=dec // nb,
        in_specs=[pl.BlockSpec(blk, here),
                  pl.BlockSpec((dec, 1, A_WIDTH), lambda i: (0, 0, mixer)),
                  pl.BlockSpec((None, dec, 1, kv_width), lambda i: (0, 0, 0, 0)),
                  pl.BlockSpec((None, dec, 1, kv_width), lambda i: (1, 0, 0, 0)),
                  _resident(kv_new_t.shape), _resident(bias_rows.shape)],
        out_specs=[pl.BlockSpec(blk, here),
                   pl.BlockSpec((dec, 1, A_WIDTH), lambda i: (0, 0, 0)),
                   pl.BlockSpec((dec, 1, 2 * LANES), lambda i: (0, 0, 0))],
        out_shape=[jax.ShapeDtypeStruct(cache_t.shape, F32),
                   jax.ShapeDtypeStruct((dec, 1, A_WIDTH), F32),
                   jax.ShapeDtypeStruct((dec, 1, 2 * LANES), F32)],
        scratch_shapes=[],
        args=(cache_t, q, kv_new, kv_new, kv_new_t, bias_rows),
        name="cache_stream",
        phases=nb * (1 + 2 * kv_heads))


def _sample_merge_body(x_ref, ng_ref, wg_ref, pa_ref, pb_ref, wo_ref, e_ref, sink_ref,
                       o1_ref, l1_ref, o2_ref, l2_ref, o3_ref, l3_ref, ob_ref, lb_ref, y_ref):
    def store(y):
        y_ref[...] = y

    for _ in _merge_math(store, x_ref[...], ng_ref, wg_ref, pa_ref, pb_ref, wo_ref, e_ref, sink_ref,
                         [o1_ref[...], o2_ref[...], o3_ref[...]],
                         [l1_ref[...], l2_ref[...], l3_ref[...]], ob_ref[...], lb_ref[...]):
        pass


def _sample_merge(x, ng, w_gates, p_a, p_b, w_o, expand, sink_row, o_list, ml_list):
    args = [x, ng, w_gates, p_a, p_b, w_o, expand, sink_row]
    for o, l in zip(o_list, ml_list):
        args += [o, l]
    return pl.pallas_call(
        _sample_merge_body,
        grid=(1,),
        in_specs=[_whole(a.shape) for a in args],
        out_specs=_whole(x.shape),
        out_shape=jax.ShapeDtypeStruct(x.shape, F32),
        compiler_params=_params(dimension_semantics=("arbitrary",)),
        name="sample_merge",
    )(*args)


def _buffer_bias_rows(window, dilation):
    r = np.arange(window)
    steps = (window - r) // dilation
    slopes = np.exp2(-np.arange(1, N_HEADS + 1, dtype=np.float32))
    bias = -slopes[:, None] * steps[None].astype(np.float32)
    return np.where((r % dilation == 0)[None], bias * LOG2E, np.float32(NEG)).astype(np.float32)


def _band_bias_table():
    q = np.arange(BLOCK)[:, None]
    kk = np.arange(2 * BLOCK)[None, :]
    delta = q + BLOCK - kk
    valid = (delta >= 0) & (delta <= BLOCK)
    slopes = np.exp2(-np.arange(1, N_HEADS + 1, dtype=np.float32))
    bias = -slopes[:, None, None] * delta[None].astype(np.float32)
    normal = np.where(valid[None], bias, np.float32(NEG))
    first = np.where((valid & (kk >= BLOCK))[None], bias, np.float32(NEG))
    return (np.stack([first, normal]) * LOG2E).astype(np.float32)


def _block_diag_ones():
    i = np.arange(MXU_WIDTH)
    return (i[:, None] // HEAD_DIM == i[None, :] // HEAD_DIM).astype(np.float32)


def _head_expand_matrix():
    e = np.zeros((LANES, A_WIDTH), np.float32)
    for pos in range(N_HEADS):
        e[_stat_lane(pos), pos * HEAD_DIM:(pos + 1) * HEAD_DIM] = 1.0
    return e


def _b_column_order():
    return np.concatenate([np.arange(HEAD_DIM) + HEAD_DIM * hd for hd in B_HEAD_ORDER])


def _prepare_weights(norm_gain, w_in, qk_norm_a, qk_norm_b, b_sinks, w_branch_a, w_branch_b, w_out):
    w = w_in[0].astype(BF16)
    order = _b_column_order()

    def reorder_heads(cols):
        half = N_HEADS // 2
        return cols.reshape(-1, 2, half, HEAD_DIM).swapaxes(1, 2).reshape(cols.shape)

    assert B_HEAD_ORDER == tuple(h for pair in zip(range(4), range(4, 8)) for h in pair)
    b_q0, z_b0 = 9 * A_WIDTH, QKV_WIDTH + A_WIDTH
    w = lax.dynamic_update_slice(w, reorder_heads(w[:, b_q0:b_q0 + A_WIDTH]), (0, b_q0))
    w = lax.dynamic_update_slice(w, reorder_heads(w[:, z_b0:z_b0 + A_WIDTH]), (0, z_b0))
    w_qkv, w_gates = w, w[:, QKV_WIDTH:]
    scale = HEAD_DIM ** -0.5 * LOG2E
    rows = []
    for g in range(3):
        rows.append(jnp.tile(qk_norm_a[0, g, 0], N_HEADS) * scale)
        rows.append(jnp.tile(qk_norm_a[0, g, 1], N_HEADS))
    rows.append(jnp.tile(qk_norm_b[0, 0], N_HEADS) * scale)
    rows.append(jnp.tile(qk_norm_b[0, 1], N_HEADS))
    gains = jnp.stack(rows).astype(F32)
    lanes = np.array([_stat_lane(pos) for pos in range(N_HEADS)])
    sink_row = jnp.zeros((1, LANES), F32).at[0, lanes].set(b_sinks[0][np.array(B_HEAD_ORDER)].astype(F32) * LOG2E)
    return dict(ng=norm_gain.astype(F32), w_qkv=w_qkv, w_gates=w_gates, gains=gains, sink_row=sink_row,
                p_a=w_branch_a[0].astype(BF16), p_b=w_branch_b[0][order].astype(BF16),
                w_o=w_out[0].astype(BF16))


def _cache_out(t):
    nb, two, width, rows = t.shape
    t = t.reshape(nb, two, width // HEAD_DIM, HEAD_DIM, rows)
    return jnp.transpose(t, (0, 4, 1, 2, 3))[None]


STREAM_BATCH = (8, 1, 1, 8)


def _layer(x_prompt, x_sample, caches, wts, consts):
    nb, seq, _ = x_prompt.shape
    dec = x_sample.shape[0]
    x2 = x_sample.reshape(dec, D_MODEL)
    q_s, kv_a, kv_b, kvt_a, kvt_b = _sample_proj(x2, wts["ng"], wts["w_qkv"], consts["bd"], wts["gains"])

    def stream_job(mixer):
        cache_t = jnp.transpose(caches[mixer][0], (0, 2, 3, 4, 1))
        kv_new, kv_new_t = (kv_b, kvt_b) if mixer == 3 else (kv_a[mixer], kvt_a[mixer])
        return _cache_stream_job(cache_t, q_s[:, None, :], mixer, kv_new[:, :, None, :], kv_new_t,
                                 consts["rows"][mixer], STREAM_BATCH[mixer])

    def attention_job(g):
        d = A_DILATIONS[g]
        shape = (nb * d, seq // d, A_WIDTH)
        return _attention_job(*(a.reshape(shape) for a in qkv[3 * g:3 * g + 3]), consts["bias"], NATURAL_ORDER)

    qkv, a3 = _run_pair(
        _prompt_proj_job(x_prompt, wts["ng"], wts["w_qkv"], consts["bd"], wts["gains"]), stream_job(2),
        interleave=False)
    prompt_caches = [_cache_out(c) for c in
                     _prompt_cache(x_prompt, wts["ng"], wts["w_qkv"], consts["bd"], wts["gains"])]
    attn = [None] * 3
    attn[0] = _run(attention_job(0))
    attn[1] = _run(attention_job(1))
    attn[2] = _run(attention_job(2))
    a1 = _run(stream_job(0))
    (o_b, ml_b), b = _run_pair(
        _attention_job(qkv[9], qkv[10], qkv[11], consts["bias"], B_HEAD_ORDER), stream_job(3),
        interleave=False)
    o_list, ml_list = [a[0] for a in attn], [a[1] for a in attn]
    (y_prompt,), a2 = _run_pair(
        _prompt_merge_job(x_prompt, wts["ng"], wts["w_gates"], wts["p_a"], wts["p_b"], wts["w_o"],
                          consts["expand"], wts["sink_row"], o_list, ml_list, o_b, ml_b), stream_job(1),
        interleave=True)

    streams = (a1, a2, a3, b)
    rows = lambda a: a.reshape(a.shape[0], a.shape[2])
    sample_caches = [jnp.transpose(s[0], (0, 4, 1, 2, 3))[None] for s in streams]
    y_sample = _sample_merge(x2, wts["ng"], wts["w_gates"], wts["p_a"], wts["p_b"], wts["w_o"],
                             consts["expand"], wts["sink_row"],
                             [rows(s[1]) for s in streams], [rows(s[2]) for s in streams])
    return y_prompt, y_sample.reshape(dec, 1, D_MODEL), prompt_caches, sample_caches


def _constants():
    assert all(B_HEAD_ORDER[pos] // (N_HEADS // B_KV_HEADS) == pos % B_KV_HEADS for pos in range(N_HEADS))
    rows = [jnp.asarray(_buffer_bias_rows(w, d)) for w, d in zip(A_WINDOWS, A_DILATIONS)]
    rows.append(jnp.asarray(_buffer_bias_rows(B_WINDOW, 1)[np.array(B_HEAD_ORDER)]))
    return dict(bias=jnp.asarray(_band_bias_table()), bd=jnp.asarray(_block_diag_ones(), BF16),
                expand=jnp.asarray(_head_expand_matrix(), BF16), rows=rows)


def kernel(x_prompt, x_sample, cache_a1_kv, cache_a2_kv, cache_a3_kv, cache_b_kv, norm_gain, w_in,
           qk_norm_a, qk_norm_b, b_sinks, w_branch_a, w_branch_b, w_out):
    wts = _prepare_weights(norm_gain, w_in, qk_norm_a, qk_norm_b, b_sinks, w_branch_a, w_branch_b, w_out)
    consts = _constants()
    y_prompt, y_sample, pc, sc = _layer(x_prompt, x_sample, (cache_a1_kv, cache_a2_kv, cache_a3_kv, cache_b_kv),
                                        wts, consts)
    return (y_prompt, y_sample, pc[0], pc[1], pc[2], pc[3], sc[0], sc[1], sc[2], sc[3])
```

```python
import functools
from typing import Callable, NamedTuple, Optional, Sequence

import numpy as np
import jax
import jax.numpy as jnp
from jax import lax
from jax.experimental import pallas as pl
from jax.experimental.pallas import tpu as pltpu

F32 = jnp.float32
BF16 = jnp.bfloat16

D_MODEL = 1024
HEAD_DIM = 64
N_HEADS = 8
A_WINDOWS = (128, 512, 2048)
A_DILATIONS = (1, 4, 16)
B_WINDOW = 128
B_KV_HEADS = 2
A_WIDTH = N_HEADS * HEAD_DIM
B_KV_WIDTH = B_KV_HEADS * HEAD_DIM
QKV_WIDTH = 9 * A_WIDTH + A_WIDTH + 2 * B_KV_WIDTH
BLOCK = 128
Q_ROWS = 2048
NORM_EPS = 1e-6
NEG = -1e30
LOG2E = float(np.log2(np.e))
LANES = 128
MXU_WIDTH = 256
TOKEN_TILE = 256
CACHE_TILE = 512
ROW_TILE = 256
VMEM_LIMIT = 60 * 1024 * 1024
B_HEAD_ORDER = (0, 4, 1, 5, 2, 6, 3, 7)
NATURAL_ORDER = tuple(range(N_HEADS))

_CHUNKS = []
for _g in range(3):
    _CHUNKS += [(1536 * _g, 512, 2 * _g), (1536 * _g + 512, 512, 2 * _g + 1), (1536 * _g + 1024, 512, None)]
_CHUNKS += [(4608, 512, 6), (5120, 128, 7), (5248, 128, None)]


def _params(**kw):
    return pltpu.CompilerParams(vmem_limit_bytes=VMEM_LIMIT, **kw)


def _resident(shape):
    nd = len(shape)
    return pl.BlockSpec(shape, lambda *_: (0,) * nd, pipeline_mode=pl.Buffered(1))


def _whole(shape):
    nd = len(shape)
    return pl.BlockSpec(shape, lambda *_: (0,) * nd)


class _Job(NamedTuple):
    body: Callable
    steps: int
    in_specs: Sequence
    out_specs: Sequence
    out_shape: Sequence
    scratch_shapes: Sequence
    args: Sequence
    name: str
    phases: int = 1


def _run(job):
    def body(*refs):
        for _ in job.body(*refs):
            pass

    return pl.pallas_call(
        body, grid=(job.steps,), in_specs=list(job.in_specs), out_specs=list(job.out_specs),
        out_shape=list(job.out_shape), scratch_shapes=list(job.scratch_shapes),
        compiler_params=_params(dimension_semantics=("arbitrary",)), name=job.name)(*job.args)


def _run_pair(a, b, interleave):
    if a.steps != b.steps:
        return _run(a), _run(b)
    n_in = (len(a.in_specs), len(b.in_specs))
    n_out = (len(a.out_specs), len(b.out_specs))
    n_scr = len(a.scratch_shapes)

    def body(*refs):
        ins, rest = refs[:sum(n_in)], refs[sum(n_in):]
        outs, scr = rest[:sum(n_out)], rest[sum(n_out):]
        gens = [a.body(*ins[:n_in[0]], *outs[:n_out[0]], *scr[:n_scr]),
                b.body(*ins[n_in[0]:], *outs[n_out[0]:], *scr[n_scr:])]
        credit = [0, 0]
        live = [True, True]
        while any(live):
            ahead = credit[0] * b.phases > credit[1] * a.phases if interleave else False
            k = 0 if (live[0] and not (live[1] and ahead)) else 1
            try:
                next(gens[k])
                credit[k] += 1
            except StopIteration:
                live[k] = False

    outs = pl.pallas_call(
        body, grid=(a.steps,), in_specs=list(a.in_specs) + list(b.in_specs),
        out_specs=list(a.out_specs) + list(b.out_specs), out_shape=list(a.out_shape) + list(b.out_shape),
        scratch_shapes=list(a.scratch_shapes) + list(b.scratch_shapes),
        compiler_params=_params(dimension_semantics=("arbitrary",)), name=a.name + "_" + b.name,
    )(*a.args, *b.args)
    return outs[:n_out[0]], outs[n_out[0]:]


def _rms_rows(x, gain_row):
    ms = jnp.mean(x * x, axis=-1, keepdims=True)
    return (x * lax.rsqrt(ms + NORM_EPS)) * gain_row


def _proj_chunk(h, w_ref, bd_ref, gain_ref, chunk):
    off, wd, gain_row = chunk
    parts = []
    for r0 in range(0, h.shape[0], ROW_TILE):
        acc = jnp.dot(h[r0:r0 + ROW_TILE], w_ref[:, off:off + wd], preferred_element_type=F32)
        if gain_row is not None:
            sq = (acc * acc).astype(BF16)
            step = min(wd, MXU_WIDTH)
            ss = jnp.concatenate(
                [jnp.dot(sq[:, c0:c0 + step], bd_ref[:step, :step], preferred_element_type=F32)
                 for c0 in range(0, wd, step)], axis=1)
            acc = (acc * lax.rsqrt(ss * (1.0 / HEAD_DIM) + NORM_EPS)) * gain_ref[gain_row:gain_row + 1, :wd]
        parts.append(acc)
    return parts[0] if len(parts) == 1 else jnp.concatenate(parts, axis=0)


def _sigmoid(z):
    return 1.0 / (1.0 + jnp.exp(-z))


def _prompt_proj_body(x_ref, ng_ref, w_ref, bd_ref, gain_ref, *refs, tt):
    outs = refs[:12]
    perm_ref = refs[12]
    h = _rms_rows(x_ref[0], ng_ref[...]).astype(BF16)
    for ci, chunk in enumerate(_CHUNKS):
        y = _proj_chunk(h, w_ref, bd_ref, gain_ref, chunk)
        mixer, role = divmod(ci, 3)
        o_ref = outs[ci]
        d = A_DILATIONS[mixer] if mixer < 3 else 1
        if mixer == 3:
            o_ref[0] = y.astype(BF16)
        elif d == 1:
            o_ref[0, 0] = y.astype(BF16)
        else:
            rows = tt // d
            for j in range(A_WIDTH // LANES):
                perm_ref[j] = y[:, j * LANES:(j + 1) * LANES]
            for r in range(d):
                for j in range(A_WIDTH // LANES):
                    o_ref[0, r, :, j * LANES:(j + 1) * LANES] = (
                        perm_ref[j, pl.ds(r, rows, stride=d), :].astype(BF16))
        yield


def _prompt_proj_job(x, ng, w_qkv, bd, gains):
    nb, seq, _ = x.shape
    tt = TOKEN_TILE
    tiles = seq // tt
    out_shape, out_specs = [], []
    for g in range(3):
        d = A_DILATIONS[g]
        for _ in range(3):
            out_shape.append(jax.ShapeDtypeStruct((nb, d, seq // d, A_WIDTH), BF16))
            out_specs.append(pl.BlockSpec((1, d, tt // d, A_WIDTH), lambda i: (i // tiles, 0, i % tiles, 0)))
    for wd in (A_WIDTH, B_KV_WIDTH, B_KV_WIDTH):
        out_shape.append(jax.ShapeDtypeStruct((nb, seq, wd), BF16))
        out_specs.append(pl.BlockSpec((1, tt, wd), lambda i: (i // tiles, i % tiles, 0)))
    return _Job(
        body=functools.partial(_prompt_proj_body, tt=tt),
        steps=nb * tiles,
        in_specs=[pl.BlockSpec((1, tt, D_MODEL), lambda i: (i // tiles, i % tiles, 0)),
                  _resident(ng.shape), _resident((D_MODEL, QKV_WIDTH)), _resident(bd.shape),
                  _resident(gains.shape)],
        out_specs=out_specs,
        out_shape=out_shape,
        scratch_shapes=[pltpu.VMEM((A_WIDTH // LANES, tt, LANES), F32)],
        args=(x, ng, w_qkv, bd, gains),
        name="prompt_proj",
        phases=len(_CHUNKS))


def _prompt_cache_body(x_ref, ng_ref, w_ref, bd_ref, gain_ref, c1_ref, c2_ref, c3_ref, cb_ref, y_scr, *, tt):
    j = pl.program_id(1)
    last = pl.num_programs(1) - 1
    h = _rms_rows(x_ref[0], ng_ref[...]).astype(BF16)

    def kv_t(rows, mixer, c):
        wd = _CHUNKS[3 * mixer + 1 + c][1]
        y_scr[:rows.shape[0], :wd] = _proj_chunk(rows, w_ref, bd_ref, gain_ref, _CHUNKS[3 * mixer + 1 + c])
        return y_scr[:rows.shape[0], :wd].T

    for c in range(2):
        c3_ref[0, c] = kv_t(h, 2, c)

    @pl.when(j == last)
    def _():
        tail = h[tt - BLOCK:]
        for c in range(2):
            c2_ref[0, c] = kv_t(h, 1, c)
            c1_ref[0, c] = kv_t(tail, 0, c)
            cb_ref[0, c] = kv_t(tail, 3, c)


def _prompt_cache(x, ng, w_qkv, bd, gains):
    nb, seq, _ = x.shape
    tt = CACHE_TILE
    assert A_WINDOWS == (BLOCK, tt, 4 * tt) and B_WINDOW == BLOCK
    n_tiles = A_WINDOWS[2] // tt
    first = seq // tt - n_tiles
    out_shape = [jax.ShapeDtypeStruct((nb, 2, A_WIDTH, w), F32) for w in A_WINDOWS]
    out_shape.append(jax.ShapeDtypeStruct((nb, 2, B_KV_WIDTH, B_WINDOW), F32))
    out_specs = [pl.BlockSpec((1, 2, A_WIDTH, BLOCK), lambda n, j: (n, 0, 0, 0)),
                 pl.BlockSpec((1, 2, A_WIDTH, tt), lambda n, j: (n, 0, 0, 0)),
                 pl.BlockSpec((1, 2, A_WIDTH, tt), lambda n, j: (n, 0, 0, j)),
                 pl.BlockSpec((1, 2, B_KV_WIDTH, BLOCK), lambda n, j: (n, 0, 0, 0))]
    return pl.pallas_call(
        functools.partial(_prompt_cache_body, tt=tt),
        grid=(nb, n_tiles),
        in_specs=[pl.BlockSpec((1, tt, D_MODEL), lambda n, j: (n, first + j, 0)),
                  _resident(ng.shape), _resident((D_MODEL, QKV_WIDTH)), _resident(bd.shape),
                  _resident(gains.shape)],
        out_specs=out_specs,
        out_shape=out_shape,
        scratch_shapes=[pltpu.VMEM((tt, A_WIDTH), F32)],
        compiler_params=_params(dimension_semantics=("arbitrary", "arbitrary")),
        name="prompt_cache",
    )(x, ng, w_qkv, bd, gains)


def _stat_lane(pos):
    return pos if pos % 2 else HEAD_DIM + pos


def _softmax_block(s, bias, v_ext):
    s = s + bias
    m = jnp.max(s, axis=-1, keepdims=True)
    p = jnp.exp2(s - m).astype(BF16)
    return m, jnp.dot(p, v_ext, preferred_element_type=F32)


def _attn_body(q_ref, kp_ref, kc_ref, vp_ref, vc_ref, bias_ref, o_ref, ml_ref,
               *, shared_kv, head_of_pos, per_seq):
    n_seq, q_rows = q_ref.shape[:2]
    n_blocks = q_rows // BLOCK
    first = jnp.minimum(pl.program_id(0) % per_seq, 1)
    lane = lax.broadcasted_iota(jnp.int32, (BLOCK, LANES), 1)
    lo = lane < HEAD_DIM
    lo_kv = lax.broadcasted_iota(jnp.int32, (BLOCK + q_rows, LANES), 1) < HEAD_DIM
    one = jnp.ones((), BF16)
    ml_ref[:, :, :LANES] = jnp.zeros((n_seq, q_rows, LANES), F32)
    ml_ref[:, :, LANES:] = jnp.ones((n_seq, q_rows, LANES), F32)

    def load_kv(si, cols):
        k_all = jnp.concatenate([kp_ref[si, :, cols], kc_ref[si, :, cols]], axis=0)
        v_all = jnp.concatenate([vp_ref[si, :, cols], vc_ref[si, :, cols]], axis=0)
        return k_all, (jnp.where(lo_kv, v_all, one), jnp.where(lo_kv, one, v_all))

    for si, i in ((si, i) for si in range(n_seq) for i in range(N_HEADS // 2)):
        cols = slice(i * LANES, (i + 1) * LANES)
        if i == 0 or not shared_kv:
            k_all, v_ext = load_kv(si, slice(0, LANES) if shared_kv else cols)
        for b in range(n_blocks):
            sel = first if b == 0 else 1
            keys = slice(b * BLOCK, (b + 2) * BLOCK)
            rows = slice(b * BLOCK, (b + 1) * BLOCK)
            qp = q_ref[si, rows, cols]
            zero = jnp.zeros_like(qp)
            q2 = jnp.concatenate([jnp.where(lo, qp, zero), jnp.where(lo, zero, qp)], axis=0)
            s2 = lax.dot_general(q2, k_all[keys], (((1,), (1,)), ((), ())), preferred_element_type=F32)
            m_pair, pv_pair = [], []
            for hh in range(2):
                m, pv = _softmax_block(s2[hh * BLOCK:(hh + 1) * BLOCK],
                                       bias_ref[sel, head_of_pos[2 * i + hh]], v_ext[hh][keys])
                m_pair.append(m)
                pv_pair.append(pv)
            o_ref[si, rows, cols] = jnp.where(lo, pv_pair[0], pv_pair[1])
            for hh in range(2):
                at = _stat_lane(2 * i + hh)
                ml_ref[si, rows, at:at + 1] = m_pair[hh]
                ml_ref[si, rows, LANES + at:LANES + at + 1] = pv_pair[hh][:, at:at + 1]
            yield


def _attention_job(q, k, v, bias, head_of_pos):
    ns, steps, _ = q.shape
    kv_width = k.shape[-1]
    q_rows = min(Q_ROWS, steps)
    n_seq = min(Q_ROWS // q_rows, ns)
    ratio = q_rows // BLOCK
    per_seq = steps // q_rows
    here = lambda i: (i // per_seq, i % per_seq, 0)
    qspec = pl.BlockSpec((n_seq, q_rows, A_WIDTH), here)
    cur = pl.BlockSpec((n_seq, q_rows, kv_width), here)
    prev = pl.BlockSpec((n_seq, BLOCK, kv_width),
                        lambda i: (i // per_seq, jnp.maximum((i % per_seq) * ratio - 1, 0), 0))
    return _Job(
        body=functools.partial(_attn_body, shared_kv=kv_width == LANES, head_of_pos=head_of_pos,
                               per_seq=per_seq),
        steps=(ns // n_seq) * per_seq,
        in_specs=[qspec, prev, cur, prev, cur, _resident(bias.shape)],
        out_specs=[qspec, pl.BlockSpec((n_seq, q_rows, 2 * LANES), here)],
        out_shape=[jax.ShapeDtypeStruct((ns, steps, A_WIDTH), F32),
                   jax.ShapeDtypeStruct((ns, steps, 2 * LANES), F32)],
        scratch_shapes=[],
        args=(q, k, k, v, v, bias),
        name="attention",
        phases=(N_HEADS // 2) * ratio * n_seq)


def _head_expand(w, e_ref):
    return jnp.dot(w.astype(BF16), e_ref[...], preferred_element_type=F32)


MERGE_PHASES = 8


def _merge_math(store, x, ng_ref, wg_ref, pa_ref, pb_ref, wo_ref, e_ref, sink_ref, o_groups, ml_groups, o_b,
                ml_b):
    ms = [ml[:, :LANES] for ml in ml_groups]
    ls = [ml[:, LANES:] for ml in ml_groups]
    big = jnp.maximum(jnp.maximum(ms[0], ms[1]), ms[2])
    es = [jnp.exp2(m - big) for m in ms]
    inv = 1.0 / (es[0] * ls[0] + es[1] * ls[1] + es[2] * ls[2])
    o_a = None
    for e, o in zip(es, o_groups):
        term = _head_expand(e * inv, e_ref) * o
        o_a = term if o_a is None else o_a + term
    inv_b = 1.0 / (ml_b[:, LANES:] + jnp.exp2(sink_ref[...] - ml_b[:, :LANES]))
    o_b = _head_expand(inv_b, e_ref) * o_b
    yield
    h = _rms_rows(x, ng_ref[...]).astype(BF16)
    gate = []
    for c0, c1 in ((0, A_WIDTH), (A_WIDTH, 2 * A_WIDTH), (2 * A_WIDTH, 2 * A_WIDTH + D_MODEL),
                   (2 * A_WIDTH + D_MODEL, 2 * A_WIDTH + 2 * D_MODEL)):
        gate.append(jnp.dot(h, wg_ref[:, c0:c1], preferred_element_type=F32))
        yield
    z_a, z_b, g_a, g_b = gate
    a = (o_a * (z_a * _sigmoid(z_a))).astype(BF16)
    b = (o_b * (z_b * _sigmoid(z_b))).astype(BF16)
    mixed = _sigmoid(g_a) * jnp.dot(a, pa_ref[...], preferred_element_type=F32)
    yield
    mixed = mixed + _sigmoid(g_b) * jnp.dot(b, pb_ref[...], preferred_element_type=F32)
    yield
    store(x + jnp.dot(mixed.astype(BF16), wo_ref[...], preferred_element_type=F32))
    yield


def _prompt_merge_body(x_ref, ng_ref, wg_ref, pa_ref, pb_ref, wo_ref, e_ref, sink_ref,
                       o1_ref, l1_ref, o2_ref, l2_ref, o3_ref, l3_ref, ob_ref, lb_ref, y_ref,
                       o_scr, l_scr, *, tt):
    for gi, (o_ref, l_ref, d) in enumerate(((o2_ref, l2_ref, 4), (o3_ref, l3_ref, 16))):
        rows = tt // d
        for r in range(d):
            for j in range(2):
                l_scr[gi, j, pl.ds(r, rows, stride=d), :] = l_ref[0, r, :, j * LANES:(j + 1) * LANES]
            for j in range(A_WIDTH // LANES):
                o_scr[gi, j, pl.ds(r, rows, stride=d), :] = o_ref[0, r, :, j * LANES:(j + 1) * LANES]
        yield
    o_groups = [o1_ref[0, 0]] + [
        jnp.concatenate([o_scr[gi, j] for j in range(A_WIDTH // LANES)], axis=1) for gi in range(2)]
    ml_groups = [l1_ref[0, 0]] + [jnp.concatenate([l_scr[gi, 0], l_scr[gi, 1]], axis=1) for gi in range(2)]

    def store(y):
        y_ref[0] = y

    yield from _merge_math(store, x_ref[0], ng_ref, wg_ref, pa_ref, pb_ref, wo_ref, e_ref, sink_ref,
                           o_groups, ml_groups, ob_ref[0], lb_ref[0])


def _prompt_merge_job(x, ng, w_gates, p_a, p_b, w_o, expand, sink_row, o_list, ml_list, o_b, ml_b):
    nb, seq, _ = x.shape
    tt = 2 * TOKEN_TILE
    tiles = seq // tt
    tile3 = lambda i: (i // tiles, i % tiles, 0)
    tile4 = lambda i: (i // tiles, 0, i % tiles, 0)
    in_specs = [pl.BlockSpec((1, tt, D_MODEL), tile3)]
    in_specs += [_resident(a.shape) for a in (ng, w_gates, p_a, p_b, w_o, expand, sink_row)]
    args = [x, ng, w_gates, p_a, p_b, w_o, expand, sink_row]
    for g in range(3):
        d = A_DILATIONS[g]
        in_specs.append(pl.BlockSpec((1, d, tt // d, A_WIDTH), tile4))
        in_specs.append(pl.BlockSpec((1, d, tt // d, 2 * LANES), tile4))
        args += [o_list[g].reshape(nb, d, seq // d, A_WIDTH), ml_list[g].reshape(nb, d, seq // d, 2 * LANES)]
    in_specs.append(pl.BlockSpec((1, tt, A_WIDTH), tile3))
    in_specs.append(pl.BlockSpec((1, tt, 2 * LANES), tile3))
    args += [o_b, ml_b]
    return _Job(
        body=functools.partial(_prompt_merge_body, tt=tt),
        steps=nb * tiles,
        in_specs=in_specs,
        out_specs=[pl.BlockSpec((1, tt, D_MODEL), tile3)],
        out_shape=[jax.ShapeDtypeStruct((nb, seq, D_MODEL), F32)],
        scratch_shapes=[pltpu.VMEM((2, A_WIDTH // LANES, tt, LANES), F32),
                        pltpu.VMEM((2, 2, tt, LANES), F32)],
        args=args,
        name="prompt_merge",
        phases=MERGE_PHASES + 2)


def _sample_proj_body(x_ref, ng_ref, w_ref, bd_ref, gain_ref, q_ref, kva_ref, kvb_ref, kvta_ref, kvtb_ref,
                      y_scr):
    dec = x_ref.shape[0]
    h = _rms_rows(x_ref[...], ng_ref[...]).astype(BF16)
    kvta_ref[...] = jnp.zeros_like(kvta_ref)
    kvtb_ref[...] = jnp.zeros_like(kvtb_ref)
    for ci, chunk in enumerate(_CHUNKS):
        wd = chunk[1]
        y = _proj_chunk(h, w_ref, bd_ref, gain_ref, chunk)
        mixer, role = divmod(ci, 3)
        if role == 0:
            q_ref[:, mixer * A_WIDTH:(mixer + 1) * A_WIDTH] = y
            continue
        y_scr[:, :wd] = y
        y_t = y_scr[:, :wd].T
        if mixer < 3:
            kva_ref[mixer, role - 1] = y
            kvta_ref[mixer, role - 1, :, :dec] = y_t
        else:
            kvb_ref[role - 1] = y
            kvtb_ref[role - 1, :, :dec] = y_t


def _sample_proj(x, ng, w_qkv, bd, gains):
    dec = x.shape[0]
    assert dec <= LANES
    out_shape = [jax.ShapeDtypeStruct((dec, 4 * A_WIDTH), F32),
                 jax.ShapeDtypeStruct((3, 2, dec, A_WIDTH), F32),
                 jax.ShapeDtypeStruct((2, dec, B_KV_WIDTH), F32),
                 jax.ShapeDtypeStruct((3, 2, A_WIDTH, LANES), F32),
                 jax.ShapeDtypeStruct((2, B_KV_WIDTH, LANES), F32)]
    args = (x, ng, w_qkv, bd, gains)
    return pl.pallas_call(
        _sample_proj_body,
        grid=(1,),
        in_specs=[_whole(x.shape), _whole(ng.shape), _whole((D_MODEL, QKV_WIDTH)), _whole(bd.shape),
                  _whole(gains.shape)],
        out_specs=[_whole(o.shape) for o in out_shape],
        out_shape=out_shape,
        scratch_shapes=[pltpu.VMEM((dec, A_WIDTH), F32)],
        compiler_params=_params(dimension_semantics=("arbitrary",)),
        name="sample_proj",
    )(*args)


def _cache_stream_body(c_ref, q_ref, kn_ref, vn_ref, kvt_ref, bias_ref, out_ref, o_ref, ml_ref, *, nb, tiles):
    window = c_ref.shape[-1]
    reps = N_HEADS // tiles
    step = pl.program_id(0)
    head_lane = lax.broadcasted_iota(jnp.int32, (N_HEADS, A_WIDTH), 1) // HEAD_DIM
    head_row = lax.broadcasted_iota(jnp.int32, (N_HEADS, A_WIDTH), 0)
    own = head_lane == head_row
    spos = lax.broadcasted_iota(jnp.int32, (N_HEADS, LANES), 0)
    slane = lax.broadcasted_iota(jnp.int32, (N_HEADS, LANES), 1)
    stat = slane == jnp.where(spos % 2 == 1, spos, spos + HEAD_DIM)
    used = jnp.sum(stat.astype(F32), axis=0, keepdims=True) > 0.0
    lane_w = lax.broadcasted_iota(jnp.int32, (HEAD_DIM, window), 1)

    def stacked(tiles_):
        return jnp.concatenate([tiles_[pos % tiles] for pos in range(N_HEADS)], axis=0)

    def tiled(row):
        return row if reps == 1 else jnp.concatenate([row] * reps, axis=1)

    for bi in range(nb):
        n = step * nb + bi
        q_row = q_ref[n]
        kn_row = tiled(kn_ref[n])
        vn_row = tiled(vn_ref[n])
        q_bd = jnp.where(own, q_row, 0.0)
        k_all = stacked([c_ref[bi, 0, g].astype(BF16) for g in range(tiles)])
        v_all = stacked([c_ref[bi, 1, g].astype(BF16) for g in range(tiles)])
        s = jnp.dot(q_bd.astype(BF16), k_all, preferred_element_type=F32) + bias_ref[...]
        s_new = jnp.sum(q_bd * kn_row, axis=1, keepdims=True)
        m = jnp.maximum(jnp.max(s, axis=1, keepdims=True), s_new)
        p = jnp.exp2(s - m)
        p_new = jnp.exp2(s_new - m)
        l = jnp.sum(p, axis=1, keepdims=True) + p_new
        o_t = lax.dot_general(p.astype(BF16), v_all, (((1,), (1,)), ((), ())), preferred_element_type=F32)
        o_ref[n] = jnp.sum(jnp.where(own, o_t + p_new * vn_row, 0.0), axis=0, keepdims=True)
        ml_ref[n, :, :LANES] = jnp.sum(jnp.where(stat, m, 0.0), axis=0, keepdims=True)
        ml_ref[n, :, LANES:] = jnp.where(used, jnp.sum(jnp.where(stat, l, 0.0), axis=0, keepdims=True), 1.0)
        yield
        for c in range(2):
            for g in range(tiles):
                new = pltpu.roll(kvt_ref[c, g * HEAD_DIM:(g + 1) * HEAD_DIM, :], LANES - 1 - n, axis=1)
                if window > LANES:
                    new = jnp.concatenate([new] * (window // LANES), axis=1)
                shifted = pltpu.roll(c_ref[bi, c, g], window - 1, axis=1)
                out_ref[bi, c, g] = jnp.where(lane_w == window - 1, new, shifted)
                yield


def _cache_stream_job(cache_t, q, mixer, kv_new, kv_new_t, bias_rows, nb):
    dec, _, kv_heads, _, window = cache_t.shape
    nb = min(nb, dec)
    blk = (nb, 2, kv_heads, HEAD_DIM, window)
    here = lambda i: (i, 0, 0, 0, 0)
    kv_width = kv_heads * HEAD_DIM
    return _Job(
        body=functools.partial(_cache_stream_body, nb=nb, tiles=kv_heads),
        steps=dec // nb,
        in_specs=[pl.BlockSpec(blk, here),
                  pl.BlockSpec((dec, 1, A_WIDTH), lambda i: (0, 0, mixer)),
                  pl.BlockSpec((None, dec, 1, kv_width), lambda i: (0, 0, 0, 0)),
                  pl.BlockSpec((None, dec, 1, kv_width), lambda i: (1, 0, 0, 0)),
                  _resident(kv_new_t.shape), _resident(bias_rows.shape)],
        out_specs=[pl.BlockSpec(blk, here),
                   pl.BlockSpec((dec, 1, A_WIDTH), lambda i: (0, 0, 0)),
                   pl.BlockSpec((dec, 1, 2 * LANES), lambda i: (0, 0, 0))],
        out_shape=[jax.ShapeDtypeStruct(cache_t.shape, F32),
                   jax.ShapeDtypeStruct((dec, 1, A_WIDTH), F32),
                   jax.ShapeDtypeStruct((dec, 1, 2 * LANES), F32)],
        scratch_shapes=[],
        args=(cache_t, q, kv_new, kv_new, kv_new_t, bias_rows),
        name="cache_stream",
        phases=nb * (1 + 2 * kv_heads))


def _sample_merge_body(x_ref, ng_ref, wg_ref, pa_ref, pb_ref, wo_ref, e_ref, sink_ref,
                       o1_ref, l1_ref, o2_ref, l2_ref, o3_ref, l3_ref, ob_ref, lb_ref, y_ref):
    def store(y):
        y_ref[...] = y

    for _ in _merge_math(store, x_ref[...], ng_ref, wg_ref, pa_ref, pb_ref, wo_ref, e_ref, sink_ref,
                         [o1_ref[...], o2_ref[...], o3_ref[...]],
                         [l1_ref[...], l2_ref[...], l3_ref[...]], ob_ref[...], lb_ref[...]):
        pass


def _sample_merge(x, ng, w_gates, p_a, p_b, w_o, expand, sink_row, o_list, ml_list):
    args = [x, ng, w_gates, p_a, p_b, w_o, expand, sink_row]
    for o, l in zip(o_list, ml_list):
        args += [o, l]
    return pl.pallas_call(
        _sample_merge_body,
        grid=(1,),
        in_specs=[_whole(a.shape) for a in args],
        out_specs=_whole(x.shape),
        out_shape=jax.ShapeDtypeStruct(x.shape, F32),
        compiler_params=_params(dimension_semantics=("arbitrary",)),
        name="sample_merge",
    )(*args)


def _buffer_bias_rows(window, dilation):
    r = np.arange(window)
    steps = (window - r) // dilation
    slopes = np.exp2(-np.arange(1, N_HEADS + 1, dtype=np.float32))
    bias = -slopes[:, None] * steps[None].astype(np.float32)
    return np.where((r % dilation == 0)[None], bias * LOG2E, np.float32(NEG)).astype(np.float32)


def _band_bias_table():
    q = np.arange(BLOCK)[:, None]
    kk = np.arange(2 * BLOCK)[None, :]
    delta = q + BLOCK - kk
    valid = (delta >= 0) & (delta <= BLOCK)
    slopes = np.exp2(-np.arange(1, N_HEADS + 1, dtype=np.float32))
    bias = -slopes[:, None, None] * delta[None].astype(np.float32)
    normal = np.where(valid[None], bias, np.float32(NEG))
    first = np.where((valid & (kk >= BLOCK))[None], bias, np.float32(NEG))
    return (np.stack([first, normal]) * LOG2E).astype(np.float32)


def _block_diag_ones():
    i = np.arange(MXU_WIDTH)
    return (i[:, None] // HEAD_DIM == i[None, :] // HEAD_DIM).astype(np.float32)


def _head_expand_matrix():
    e = np.zeros((LANES, A_WIDTH), np.float32)
    for pos in range(N_HEADS):
        e[_stat_lane(pos), pos * HEAD_DIM:(pos + 1) * HEAD_DIM] = 1.0
    return e


def _b_column_order():
    return np.concatenate([np.arange(HEAD_DIM) + HEAD_DIM * hd for hd in B_HEAD_ORDER])


def _prepare_weights(norm_gain, w_in, qk_norm_a, qk_norm_b, b_sinks, w_branch_a, w_branch_b, w_out):
    w = w_in[0].astype(BF16)
    order = _b_column_order()

    def reorder_heads(cols):
        half = N_HEADS // 2
        return cols.reshape(-1, 2, half, HEAD_DIM).swapaxes(1, 2).reshape(cols.shape)

    assert B_HEAD_ORDER == tuple(h for pair in zip(range(4), range(4, 8)) for h in pair)
    b_q0, z_b0 = 9 * A_WIDTH, QKV_WIDTH + A_WIDTH
    w = lax.dynamic_update_slice(w, reorder_heads(w[:, b_q0:b_q0 + A_WIDTH]), (0, b_q0))
    w = lax.dynamic_update_slice(w, reorder_heads(w[:, z_b0:z_b0 + A_WIDTH]), (0, z_b0))
    w_qkv, w_gates = w, w[:, QKV_WIDTH:]
    scale = HEAD_DIM ** -0.5 * LOG2E
    rows = []
    for g in range(3):
        rows.append(jnp.tile(qk_norm_a[0, g, 0], N_HEADS) * scale)
        rows.append(jnp.tile(qk_norm_a[0, g, 1], N_HEADS))
    rows.append(jnp.tile(qk_norm_b[0, 0], N_HEADS) * scale)
    rows.append(jnp.tile(qk_norm_b[0, 1], N_HEADS))
    gains = jnp.stack(rows).astype(F32)
    lanes = np.array([_stat_lane(pos) for pos in range(N_HEADS)])
    sink_row = jnp.zeros((1, LANES), F32).at[0, lanes].set(b_sinks[0][np.array(B_HEAD_ORDER)].astype(F32) * LOG2E)
    return dict(ng=norm_gain.astype(F32), w_qkv=w_qkv, w_gates=w_gates, gains=gains, sink_row=sink_row,
                p_a=w_branch_a[0].astype(BF16), p_b=w_branch_b[0][order].astype(BF16),
                w_o=w_out[0].astype(BF16))


def _cache_out(t):
    nb, two, width, rows = t.shape
    t = t.reshape(nb, two, width // HEAD_DIM, HEAD_DIM, rows)
    return jnp.transpose(t, (0, 4, 1, 2, 3))[None]


STREAM_BATCH = (8, 2, 1, 8)


def _layer(x_prompt, x_sample, caches, wts, consts):
    nb, seq, _ = x_prompt.shape
    dec = x_sample.shape[0]
    x2 = x_sample.reshape(dec, D_MODEL)
    q_s, kv_a, kv_b, kvt_a, kvt_b = _sample_proj(x2, wts["ng"], wts["w_qkv"], consts["bd"], wts["gains"])

    def stream_job(mixer):
        cache_t = jnp.transpose(caches[mixer][0], (0, 2, 3, 4, 1))
        kv_new, kv_new_t = (kv_b, kvt_b) if mixer == 3 else (kv_a[mixer], kvt_a[mixer])
        return _cache_stream_job(cache_t, q_s[:, None, :], mixer, kv_new[:, :, None, :], kv_new_t,
                                 consts["rows"][mixer], STREAM_BATCH[mixer])

    def attention_job(g):
        d = A_DILATIONS[g]
        shape = (nb * d, seq // d, A_WIDTH)
        return _attention_job(*(a.reshape(shape) for a in qkv[3 * g:3 * g + 3]), consts["bias"], NATURAL_ORDER)

    qkv, a3 = _run_pair(
        _prompt_proj_job(x_prompt, wts["ng"], wts["w_qkv"], consts["bd"], wts["gains"]), stream_job(2),
        interleave=False)
    prompt_caches = [_cache_out(c) for c in
                     _prompt_cache(x_prompt, wts["ng"], wts["w_qkv"], consts["bd"], wts["gains"])]
    attn = [None] * 3
    attn[0] = _run(attention_job(0))
    attn[1] = _run(attention_job(1))
    attn[2] = _run(attention_job(2))
    a1 = _run(stream_job(0))
    (o_b, ml_b), b = _run_pair(
        _attention_job(qkv[9], qkv[10], qkv[11], consts["bias"], B_HEAD_ORDER), stream_job(3),
        interleave=False)
    o_list, ml_list = [a[0] for a in attn], [a[1] for a in attn]
    (y_prompt,), a2 = _run_pair(
        _prompt_merge_job(x_prompt, wts["ng"], wts["w_gates"], wts["p_a"], wts["p_b"], wts["w_o"],
                          consts["expand"], wts["sink_row"], o_list, ml_list, o_b, ml_b), stream_job(1),
        interleave=True)

    streams = (a1, a2, a3, b)
    rows = lambda a: a.reshape(a.shape[0], a.shape[2])
    sample_caches = [jnp.transpose(s[0], (0, 4, 1, 2, 3))[None] for s in streams]
    y_sample = _sample_merge(x2, wts["ng"], wts["w_gates"], wts["p_a"], wts["p_b"], wts["w_o"],
                             consts["expand"], wts["sink_row"],
                             [rows(s[1]) for s in streams], [rows(s[2]) for s in streams])
    return y_prompt, y_sample.reshape(dec, 1, D_MODEL), prompt_caches, sample_caches


def _constants():
    assert all(B_HEAD_ORDER[pos] // (N_HEADS // B_KV_HEADS) == pos % B_KV_HEADS for pos in range(N_HEADS))
    rows = [jnp.asarray(_buffer_bias_rows(w, d)) for w, d in zip(A_WINDOWS, A_DILATIONS)]
    rows.append(jnp.asarray(_buffer_bias_rows(B_WINDOW, 1)[np.array(B_HEAD_ORDER)]))
    return dict(bias=jnp.asarray(_band_bias_table()), bd=jnp.asarray(_block_diag_ones(), BF16),
                expand=jnp.asarray(_head_expand_matrix(), BF16), rows=rows)


def kernel(x_prompt, x_sample, cache_a1_kv, cache_a2_kv, cache_a3_kv, cache_b_kv, norm_gain, w_in,
           qk_norm_a, qk_norm_b, b_sinks, w_branch_a, w_branch_b, w_out):
    wts = _prepare_weights(norm_gain, w_in, qk_norm_a, qk_norm_b, b_sinks, w_branch_a, w_branch_b, w_out)
    consts = _constants()
    y_prompt, y_sample, pc, sc = _layer(x_prompt, x_sample, (cache_a1_kv, cache_a2_kv, cache_a3_kv, cache_b_kv),
                                        wts, consts)
    return (y_prompt, y_sample, pc[0], pc[1], pc[2], pc[3], sc[0], sc[1], sc[2], sc[3])
```

```python
import functools
from typing import Callable, NamedTuple, Optional, Sequence

import numpy as np
import jax
import jax.numpy as jnp
from jax import lax
from jax.experimental import pallas as pl
from jax.experimental.pallas import tpu as pltpu

F32 = jnp.float32
BF16 = jnp.bfloat16

D_MODEL = 1024
HEAD_DIM = 64
N_HEADS = 8
A_WINDOWS = (128, 512, 2048)
A_DILATIONS = (1, 4, 16)
B_WINDOW = 128
B_KV_HEADS = 2
A_WIDTH = N_HEADS * HEAD_DIM
B_KV_WIDTH = B_KV_HEADS * HEAD_DIM
QKV_WIDTH = 9 * A_WIDTH + A_WIDTH + 2 * B_KV_WIDTH
BLOCK = 128
Q_ROWS = 4096
NORM_EPS = 1e-6
NEG = -1e30
LOG2E = float(np.log2(np.e))
LANES = 128
MXU_WIDTH = 256
TOKEN_TILE = 256
CACHE_TILE = 512
ROW_TILE = 256
VMEM_LIMIT = 60 * 1024 * 1024
B_HEAD_ORDER = (0, 4, 1, 5, 2, 6, 3, 7)
NATURAL_ORDER = tuple(range(N_HEADS))

_CHUNKS = []
for _g in range(3):
    _CHUNKS += [(1536 * _g, 512, 2 * _g), (1536 * _g + 512, 512, 2 * _g + 1), (1536 * _g + 1024, 512, None)]
_CHUNKS += [(4608, 512, 6), (5120, 128, 7), (5248, 128, None)]


def _params(**kw):
    return pltpu.CompilerParams(vmem_limit_bytes=VMEM_LIMIT, **kw)


def _resident(shape):
    nd = len(shape)
    return pl.BlockSpec(shape, lambda *_: (0,) * nd, pipeline_mode=pl.Buffered(1))


def _whole(shape):
    nd = len(shape)
    return pl.BlockSpec(shape, lambda *_: (0,) * nd)


class _Job(NamedTuple):
    body: Callable
    steps: int
    in_specs: Sequence
    out_specs: Sequence
    out_shape: Sequence
    scratch_shapes: Sequence
    args: Sequence
    name: str
    phases: int = 1


def _run(job):
    def body(*refs):
        for _ in job.body(*refs):
            pass

    return pl.pallas_call(
        body, grid=(job.steps,), in_specs=list(job.in_specs), out_specs=list(job.out_specs),
        out_shape=list(job.out_shape), scratch_shapes=list(job.scratch_shapes),
        compiler_params=_params(dimension_semantics=("arbitrary",)), name=job.name)(*job.args)


def _run_pair(a, b, interleave):
    if a.steps != b.steps:
        return _run(a), _run(b)
    n_in = (len(a.in_specs), len(b.in_specs))
    n_out = (len(a.out_specs), len(b.out_specs))
    n_scr = len(a.scratch_shapes)

    def body(*refs):
        ins, rest = refs[:sum(n_in)], refs[sum(n_in):]
        outs, scr = rest[:sum(n_out)], rest[sum(n_out):]
        gens = [a.body(*ins[:n_in[0]], *outs[:n_out[0]], *scr[:n_scr]),
                b.body(*ins[n_in[0]:], *outs[n_out[0]:], *scr[n_scr:])]
        credit = [0, 0]
        live = [True, True]
        while any(live):
            ahead = credit[0] * b.phases > credit[1] * a.phases if interleave else False
            k = 0 if (live[0] and not (live[1] and ahead)) else 1
            try:
                next(gens[k])
                credit[k] += 1
            except StopIteration:
                live[k] = False

    outs = pl.pallas_call(
        body, grid=(a.steps,), in_specs=list(a.in_specs) + list(b.in_specs),
        out_specs=list(a.out_specs) + list(b.out_specs), out_shape=list(a.out_shape) + list(b.out_shape),
        scratch_shapes=list(a.scratch_shapes) + list(b.scratch_shapes),
        compiler_params=_params(dimension_semantics=("arbitrary",)), name=a.name + "_" + b.name,
    )(*a.args, *b.args)
    return outs[:n_out[0]], outs[n_out[0]:]


def _rms_rows(x, gain_row):
    ms = jnp.mean(x * x, axis=-1, keepdims=True)
    return (x * lax.rsqrt(ms + NORM_EPS)) * gain_row


def _proj_chunk(h, w_ref, bd_ref, gain_ref, chunk):
    off, wd, gain_row = chunk
    parts = []
    for r0 in range(0, h.shape[0], ROW_TILE):
        acc = jnp.dot(h[r0:r0 + ROW_TILE], w_ref[:, off:off + wd], preferred_element_type=F32)
        if gain_row is not None:
            sq = (acc * acc).astype(BF16)
            step = min(wd, MXU_WIDTH)
            ss = jnp.concatenate(
                [jnp.dot(sq[:, c0:c0 + step], bd_ref[:step, :step], preferred_element_type=F32)
                 for c0 in range(0, wd, step)], axis=1)
            acc = (acc * lax.rsqrt(ss * (1.0 / HEAD_DIM) + NORM_EPS)) * gain_ref[gain_row:gain_row + 1, :wd]
        parts.append(acc)
    return parts[0] if len(parts) == 1 else jnp.concatenate(parts, axis=0)


def _sigmoid(z):
    return 1.0 / (1.0 + jnp.exp(-z))


def _prompt_proj_body(x_ref, ng_ref, w_ref, bd_ref, gain_ref, *refs, tt):
    outs = refs[:12]
    perm_ref = refs[12]
    h = _rms_rows(x_ref[0], ng_ref[...]).astype(BF16)
    for ci, chunk in enumerate(_CHUNKS):
        y = _proj_chunk(h, w_ref, bd_ref, gain_ref, chunk)
        mixer, role = divmod(ci, 3)
        o_ref = outs[ci]
        d = A_DILATIONS[mixer] if mixer < 3 else 1
        if mixer == 3:
            o_ref[0] = y.astype(BF16)
        elif d == 1:
            o_ref[0, 0] = y.astype(BF16)
        else:
            rows = tt // d
            for j in range(A_WIDTH // LANES):
                perm_ref[j] = y[:, j * LANES:(j + 1) * LANES]
            for r in range(d):
                for j in range(A_WIDTH // LANES):
                    o_ref[0, r, :, j * LANES:(j + 1) * LANES] = (
                        perm_ref[j, pl.ds(r, rows, stride=d), :].astype(BF16))
        yield


def _prompt_proj_job(x, ng, w_qkv, bd, gains):
    nb, seq, _ = x.shape
    tt = TOKEN_TILE
    tiles = seq // tt
    out_shape, out_specs = [], []
    for g in range(3):
        d = A_DILATIONS[g]
        for _ in range(3):
            out_shape.append(jax.ShapeDtypeStruct((nb, d, seq // d, A_WIDTH), BF16))
            out_specs.append(pl.BlockSpec((1, d, tt // d, A_WIDTH), lambda i: (i // tiles, 0, i % tiles, 0)))
    for wd in (A_WIDTH, B_KV_WIDTH, B_KV_WIDTH):
        out_shape.append(jax.ShapeDtypeStruct((nb, seq, wd), BF16))
        out_specs.append(pl.BlockSpec((1, tt, wd), lambda i: (i // tiles, i % tiles, 0)))
    return _Job(
        body=functools.partial(_prompt_proj_body, tt=tt),
        steps=nb * tiles,
        in_specs=[pl.BlockSpec((1, tt, D_MODEL), lambda i: (i // tiles, i % tiles, 0)),
                  _resident(ng.shape), _resident((D_MODEL, QKV_WIDTH)), _resident(bd.shape),
                  _resident(gains.shape)],
        out_specs=out_specs,
        out_shape=out_shape,
        scratch_shapes=[pltpu.VMEM((A_WIDTH // LANES, tt, LANES), F32)],
        args=(x, ng, w_qkv, bd, gains),
        name="prompt_proj",
        phases=len(_CHUNKS))


def _prompt_cache_body(x_ref, ng_ref, w_ref, bd_ref, gain_ref, c1_ref, c2_ref, c3_ref, cb_ref, y_scr, *, tt):
    j = pl.program_id(1)
    last = pl.num_programs(1) - 1
    h = _rms_rows(x_ref[0], ng_ref[...]).astype(BF16)

    def kv_t(rows, mixer, c):
        wd = _CHUNKS[3 * mixer + 1 + c][1]
        y_scr[:rows.shape[0], :wd] = _proj_chunk(rows, w_ref, bd_ref, gain_ref, _CHUNKS[3 * mixer + 1 + c])
        return y_scr[:rows.shape[0], :wd].T

    for c in range(2):
        c3_ref[0, c] = kv_t(h, 2, c)

    @pl.when(j == last)
    def _():
        tail = h[tt - BLOCK:]
        for c in range(2):
            c2_ref[0, c] = kv_t(h, 1, c)
            c1_ref[0, c] = kv_t(tail, 0, c)
            cb_ref[0, c] = kv_t(tail, 3, c)


def _prompt_cache(x, ng, w_qkv, bd, gains):
    nb, seq, _ = x.shape
    tt = CACHE_TILE
    assert A_WINDOWS == (BLOCK, tt, 4 * tt) and B_WINDOW == BLOCK
    n_tiles = A_WINDOWS[2] // tt
    first = seq // tt - n_tiles
    out_shape = [jax.ShapeDtypeStruct((nb, 2, A_WIDTH, w), F32) for w in A_WINDOWS]
    out_shape.append(jax.ShapeDtypeStruct((nb, 2, B_KV_WIDTH, B_WINDOW), F32))
    out_specs = [pl.BlockSpec((1, 2, A_WIDTH, BLOCK), lambda n, j: (n, 0, 0, 0)),
                 pl.BlockSpec((1, 2, A_WIDTH, tt), lambda n, j: (n, 0, 0, 0)),
                 pl.BlockSpec((1, 2, A_WIDTH, tt), lambda n, j: (n, 0, 0, j)),
                 pl.BlockSpec((1, 2, B_KV_WIDTH, BLOCK), lambda n, j: (n, 0, 0, 0))]
    return pl.pallas_call(
        functools.partial(_prompt_cache_body, tt=tt),
        grid=(nb, n_tiles),
        in_specs=[pl.BlockSpec((1, tt, D_MODEL), lambda n, j: (n, first + j, 0)),
                  _resident(ng.shape), _resident((D_MODEL, QKV_WIDTH)), _resident(bd.shape),
                  _resident(gains.shape)],
        out_specs=out_specs,
        out_shape=out_shape,
        scratch_shapes=[pltpu.VMEM((tt, A_WIDTH), F32)],
        compiler_params=_params(dimension_semantics=("arbitrary", "arbitrary")),
        name="prompt_cache",
    )(x, ng, w_qkv, bd, gains)


def _stat_lane(pos):
    return pos if pos % 2 else HEAD_DIM + pos


def _softmax_block(s, bias, v_ext):
    s = s + bias
    m = jnp.max(s, axis=-1, keepdims=True)
    p = jnp.exp2(s - m).astype(BF16)
    return m, jnp.dot(p, v_ext, preferred_element_type=F32)


def _attn_body(q_ref, kp_ref, kc_ref, vp_ref, vc_ref, bias_ref, o_ref, ml_ref,
               *, shared_kv, head_of_pos, per_seq):
    n_seq, q_rows = q_ref.shape[:2]
    n_blocks = q_rows // BLOCK
    first = jnp.minimum(pl.program_id(0) % per_seq, 1)
    lane = lax.broadcasted_iota(jnp.int32, (BLOCK, LANES), 1)
    lo = lane < HEAD_DIM
    lo_kv = lax.broadcasted_iota(jnp.int32, (BLOCK + q_rows, LANES), 1) < HEAD_DIM
    one = jnp.ones((), BF16)
    ml_ref[:, :, :LANES] = jnp.zeros((n_seq, q_rows, LANES), F32)
    ml_ref[:, :, LANES:] = jnp.ones((n_seq, q_rows, LANES), F32)

    def load_kv(si, cols):
        k_all = jnp.concatenate([kp_ref[si, :, cols], kc_ref[si, :, cols]], axis=0)
        v_all = jnp.concatenate([vp_ref[si, :, cols], vc_ref[si, :, cols]], axis=0)
        return k_all, (jnp.where(lo_kv, v_all, one), jnp.where(lo_kv, one, v_all))

    for si, i in ((si, i) for si in range(n_seq) for i in range(N_HEADS // 2)):
        cols = slice(i * LANES, (i + 1) * LANES)
        if i == 0 or not shared_kv:
            k_all, v_ext = load_kv(si, slice(0, LANES) if shared_kv else cols)
        for b in range(n_blocks):
            sel = first if b == 0 else 1
            keys = slice(b * BLOCK, (b + 2) * BLOCK)
            rows = slice(b * BLOCK, (b + 1) * BLOCK)
            qp = q_ref[si, rows, cols]
            zero = jnp.zeros_like(qp)
            q2 = jnp.concatenate([jnp.where(lo, qp, zero), jnp.where(lo, zero, qp)], axis=0)
            s2 = lax.dot_general(q2, k_all[keys], (((1,), (1,)), ((), ())), preferred_element_type=F32)
            m_pair, pv_pair = [], []
            for hh in range(2):
                m, pv = _softmax_block(s2[hh * BLOCK:(hh + 1) * BLOCK],
                                       bias_ref[sel, head_of_pos[2 * i + hh]], v_ext[hh][keys])
                m_pair.append(m)
                pv_pair.append(pv)
            o_ref[si, rows, cols] = jnp.where(lo, pv_pair[0], pv_pair[1])
            for hh in range(2):
                at = _stat_lane(2 * i + hh)
                ml_ref[si, rows, at:at + 1] = m_pair[hh]
                ml_ref[si, rows, LANES + at:LANES + at + 1] = pv_pair[hh][:, at:at + 1]
            yield


def _attention_job(q, k, v, bias, head_of_pos):
    ns, steps, _ = q.shape
    kv_width = k.shape[-1]
    q_rows = min(Q_ROWS, steps)
    n_seq = min(Q_ROWS // q_rows, ns)
    ratio = q_rows // BLOCK
    per_seq = steps // q_rows
    here = lambda i: (i // per_seq, i % per_seq, 0)
    qspec = pl.BlockSpec((n_seq, q_rows, A_WIDTH), here)
    cur = pl.BlockSpec((n_seq, q_rows, kv_width), here)
    prev = pl.BlockSpec((n_seq, BLOCK, kv_width),
                        lambda i: (i // per_seq, jnp.maximum((i % per_seq) * ratio - 1, 0), 0))
    return _Job(
        body=functools.partial(_attn_body, shared_kv=kv_width == LANES, head_of_pos=head_of_pos,
                               per_seq=per_seq),
        steps=(ns // n_seq) * per_seq,
        in_specs=[qspec, prev, cur, prev, cur, _resident(bias.shape)],
        out_specs=[qspec, pl.BlockSpec((n_seq, q_rows, 2 * LANES), here)],
        out_shape=[jax.ShapeDtypeStruct((ns, steps, A_WIDTH), F32),
                   jax.ShapeDtypeStruct((ns, steps, 2 * LANES), F32)],
        scratch_shapes=[],
        args=(q, k, k, v, v, bias),
        name="attention",
        phases=(N_HEADS // 2) * ratio * n_seq)


def _head_expand(w, e_ref):
    return jnp.dot(w.astype(BF16), e_ref[...], preferred_element_type=F32)


MERGE_PHASES = 8


def _merge_math(store, x, ng_ref, wg_ref, pa_ref, pb_ref, wo_ref, e_ref, sink_ref, o_groups, ml_groups, o_b,
                ml_b):
    ms = [ml[:, :LANES] for ml in ml_groups]
    ls = [ml[:, LANES:] for ml in ml_groups]
    big = jnp.maximum(jnp.maximum(ms[0], ms[1]), ms[2])
    es = [jnp.exp2(m - big) for m in ms]
    inv = 1.0 / (es[0] * ls[0] + es[1] * ls[1] + es[2] * ls[2])
    o_a = None
    for e, o in zip(es, o_groups):
        term = _head_expand(e * inv, e_ref) * o
        o_a = term if o_a is None else o_a + term
    inv_b = 1.0 / (ml_b[:, LANES:] + jnp.exp2(sink_ref[...] - ml_b[:, :LANES]))
    o_b = _head_expand(inv_b, e_ref) * o_b
    yield
    h = _rms_rows(x, ng_ref[...]).astype(BF16)
    gate = []
    for c0, c1 in ((0, A_WIDTH), (A_WIDTH, 2 * A_WIDTH), (2 * A_WIDTH, 2 * A_WIDTH + D_MODEL),
                   (2 * A_WIDTH + D_MODEL, 2 * A_WIDTH + 2 * D_MODEL)):
        gate.append(jnp.dot(h, wg_ref[:, c0:c1], preferred_element_type=F32))
        yield
    z_a, z_b, g_a, g_b = gate
    a = (o_a * (z_a * _sigmoid(z_a))).astype(BF16)
    b = (o_b * (z_b * _sigmoid(z_b))).astype(BF16)
    mixed = _sigmoid(g_a) * jnp.dot(a, pa_ref[...], preferred_element_type=F32)
    yield
    mixed = mixed + _sigmoid(g_b) * jnp.dot(b, pb_ref[...], preferred_element_type=F32)
    yield
    store(x + jnp.dot(mixed.astype(BF16), wo_ref[...], preferred_element_type=F32))
    yield


def _prompt_merge_body(x_ref, ng_ref, wg_ref, pa_ref, pb_ref, wo_ref, e_ref, sink_ref,
                       o1_ref, l1_ref, o2_ref, l2_ref, o3_ref, l3_ref, ob_ref, lb_ref, y_ref,
                       o_scr, l_scr, *, tt):
    for gi, (o_ref, l_ref, d) in enumerate(((o2_ref, l2_ref, 4), (o3_ref, l3_ref, 16))):
        rows = tt // d
        for r in range(d):
            for j in range(2):
                l_scr[gi, j, pl.ds(r, rows, stride=d), :] = l_ref[0, r, :, j * LANES:(j + 1) * LANES]
            for j in range(A_WIDTH // LANES):
                o_scr[gi, j, pl.ds(r, rows, stride=d), :] = o_ref[0, r, :, j * LANES:(j + 1) * LANES]
        yield
    o_groups = [o1_ref[0, 0]] + [
        jnp.concatenate([o_scr[gi, j] for j in range(A_WIDTH // LANES)], axis=1) for gi in range(2)]
    ml_groups = [l1_ref[0, 0]] + [jnp.concatenate([l_scr[gi, 0], l_scr[gi, 1]], axis=1) for gi in range(2)]

    def store(y):
        y_ref[0] = y

    yield from _merge_math(store, x_ref[0], ng_ref, wg_ref, pa_ref, pb_ref, wo_ref, e_ref, sink_ref,
                           o_groups, ml_groups, ob_ref[0], lb_ref[0])


def _prompt_merge_job(x, ng, w_gates, p_a, p_b, w_o, expand, sink_row, o_list, ml_list, o_b, ml_b):
    nb, seq, _ = x.shape
    tt = 2 * TOKEN_TILE
    tiles = seq // tt
    tile3 = lambda i: (i // tiles, i % tiles, 0)
    tile4 = lambda i: (i // tiles, 0, i % tiles, 0)
    in_specs = [pl.BlockSpec((1, tt, D_MODEL), tile3)]
    in_specs += [_resident(a.shape) for a in (ng, w_gates, p_a, p_b, w_o, expand, sink_row)]
    args = [x, ng, w_gates, p_a, p_b, w_o, expand, sink_row]
    for g in range(3):
        d = A_DILATIONS[g]
        in_specs.append(pl.BlockSpec((1, d, tt // d, A_WIDTH), tile4))
        in_specs.append(pl.BlockSpec((1, d, tt // d, 2 * LANES), tile4))
        args += [o_list[g].reshape(nb, d, seq // d, A_WIDTH), ml_list[g].reshape(nb, d, seq // d, 2 * LANES)]
    in_specs.append(pl.BlockSpec((1, tt, A_WIDTH), tile3))
    in_specs.append(pl.BlockSpec((1, tt, 2 * LANES), tile3))
    args += [o_b, ml_b]
    return _Job(
        body=functools.partial(_prompt_merge_body, tt=tt),
        steps=nb * tiles,
        in_specs=in_specs,
        out_specs=[pl.BlockSpec((1, tt, D_MODEL), tile3)],
        out_shape=[jax.ShapeDtypeStruct((nb, seq, D_MODEL), F32)],
        scratch_shapes=[pltpu.VMEM((2, A_WIDTH // LANES, tt, LANES), F32),
                        pltpu.VMEM((2, 2, tt, LANES), F32)],
        args=args,
        name="prompt_merge",
        phases=MERGE_PHASES + 2)


def _sample_proj_body(x_ref, ng_ref, w_ref, bd_ref, gain_ref, q_ref, kva_ref, kvb_ref, kvta_ref, kvtb_ref,
                      y_scr):
    dec = x_ref.shape[0]
    h = _rms_rows(x_ref[...], ng_ref[...]).astype(BF16)
    kvta_ref[...] = jnp.zeros_like(kvta_ref)
    kvtb_ref[...] = jnp.zeros_like(kvtb_ref)
    for ci, chunk in enumerate(_CHUNKS):
        wd = chunk[1]
        y = _proj_chunk(h, w_ref, bd_ref, gain_ref, chunk)
        mixer, role = divmod(ci, 3)
        if role == 0:
            q_ref[:, mixer * A_WIDTH:(mixer + 1) * A_WIDTH] = y
            continue
        y_scr[:, :wd] = y
        y_t = y_scr[:, :wd].T
        if mixer < 3:
            kva_ref[mixer, role - 1] = y
            kvta_ref[mixer, role - 1, :, :dec] = y_t
        else:
            kvb_ref[role - 1] = y
            kvtb_ref[role - 1, :, :dec] = y_t


def _sample_proj(x, ng, w_qkv, bd, gains):
    dec = x.shape[0]
    assert dec <= LANES
    out_shape = [jax.ShapeDtypeStruct((dec, 4 * A_WIDTH), F32),
                 jax.ShapeDtypeStruct((3, 2, dec, A_WIDTH), F32),
                 jax.ShapeDtypeStruct((2, dec, B_KV_WIDTH), F32),
                 jax.ShapeDtypeStruct((3, 2, A_WIDTH, LANES), F32),
                 jax.ShapeDtypeStruct((2, B_KV_WIDTH, LANES), F32)]
    args = (x, ng, w_qkv, bd, gains)
    return pl.pallas_call(
        _sample_proj_body,
        grid=(1,),
        in_specs=[_whole(x.shape), _whole(ng.shape), _whole((D_MODEL, QKV_WIDTH)), _whole(bd.shape),
                  _whole(gains.shape)],
        out_specs=[_whole(o.shape) for o in out_shape],
        out_shape=out_shape,
        scratch_shapes=[pltpu.VMEM((dec, A_WIDTH), F32)],
        compiler_params=_params(dimension_semantics=("arbitrary",)),
        name="sample_proj",
    )(*args)


def _cache_stream_body(c_ref, q_ref, kn_ref, vn_ref, kvt_ref, bias_ref, out_ref, o_ref, ml_ref, *, nb, tiles):
    window = c_ref.shape[-1]
    reps = N_HEADS // tiles
    step = pl.program_id(0)
    head_lane = lax.broadcasted_iota(jnp.int32, (N_HEADS, A_WIDTH), 1) // HEAD_DIM
    head_row = lax.broadcasted_iota(jnp.int32, (N_HEADS, A_WIDTH), 0)
    own = head_lane == head_row
    spos = lax.broadcasted_iota(jnp.int32, (N_HEADS, LANES), 0)
    slane = lax.broadcasted_iota(jnp.int32, (N_HEADS, LANES), 1)
    stat = slane == jnp.where(spos % 2 == 1, spos, spos + HEAD_DIM)
    used = jnp.sum(stat.astype(F32), axis=0, keepdims=True) > 0.0
    lane_w = lax.broadcasted_iota(jnp.int32, (HEAD_DIM, window), 1)

    def stacked(tiles_):
        return jnp.concatenate([tiles_[pos % tiles] for pos in range(N_HEADS)], axis=0)

    def tiled(row):
        return row if reps == 1 else jnp.concatenate([row] * reps, axis=1)

    for bi in range(nb):
        n = step * nb + bi
        q_row = q_ref[n]
        kn_row = tiled(kn_ref[n])
        vn_row = tiled(vn_ref[n])
        q_bd = jnp.where(own, q_row, 0.0)
        k_all = stacked([c_ref[bi, 0, g].astype(BF16) for g in range(tiles)])
        v_all = stacked([c_ref[bi, 1, g].astype(BF16) for g in range(tiles)])
        s = jnp.dot(q_bd.astype(BF16), k_all, preferred_element_type=F32) + bias_ref[...]
        s_new = jnp.sum(q_bd * kn_row, axis=1, keepdims=True)
        m = jnp.maximum(jnp.max(s, axis=1, keepdims=True), s_new)
        p = jnp.exp2(s - m)
        p_new = jnp.exp2(s_new - m)
        l = jnp.sum(p, axis=1, keepdims=True) + p_new
        o_t = lax.dot_general(p.astype(BF16), v_all, (((1,), (1,)), ((), ())), preferred_element_type=F32)
        o_ref[n] = jnp.sum(jnp.where(own, o_t + p_new * vn_row, 0.0), axis=0, keepdims=True)
        ml_ref[n, :, :LANES] = jnp.sum(jnp.where(stat, m, 0.0), axis=0, keepdims=True)
        ml_ref[n, :, LANES:] = jnp.where(used, jnp.sum(jnp.where(stat, l, 0.0), axis=0, keepdims=True), 1.0)
        yield
        for c in range(2):
            for g in range(tiles):
                new = pltpu.roll(kvt_ref[c, g * HEAD_DIM:(g + 1) * HEAD_DIM, :], LANES - 1 - n, axis=1)
                if window > LANES:
                    new = jnp.concatenate([new] * (window // LANES), axis=1)
                shifted = pltpu.roll(c_ref[bi, c, g], window - 1, axis=1)
                out_ref[bi, c, g] = jnp.where(lane_w == window - 1, new, shifted)
                yield


def _cache_stream_job(cache_t, q, mixer, kv_new, kv_new_t, bias_rows, nb):
    dec, _, kv_heads, _, window = cache_t.shape
    nb = min(nb, dec)
    blk = (nb, 2, kv_heads, HEAD_DIM, window)
    here = lambda i: (i, 0, 0, 0, 0)
    kv_width = kv_heads * HEAD_DIM
    return _Job(
        body=functools.partial(_cache_stream_body, nb=nb, tiles=kv_heads),
        steps=dec // nb,
        in_specs=[pl.BlockSpec(blk, here),
                  pl.BlockSpec((dec, 1, A_WIDTH), lambda i: (0, 0, mixer)),
                  pl.BlockSpec((None, dec, 1, kv_width), lambda i: (0, 0, 0, 0)),
                  pl.BlockSpec((None, dec, 1, kv_width), lambda i: (1, 0, 0, 0)),
                  _resident(kv_new_t.shape), _resident(bias_rows.shape)],
        out_specs=[pl.BlockSpec(blk, here),
                   pl.BlockSpec((dec, 1, A_WIDTH), lambda i: (0, 0, 0)),
                   pl.BlockSpec((dec, 1, 2 * LANES), lambda i: (0, 0, 0))],
        out_shape=[jax.ShapeDtypeStruct(cache_t.shape, F32),
                   jax.ShapeDtypeStruct((dec, 1, A_WIDTH), F32),
                   jax.ShapeDtypeStruct((dec, 1, 2 * LANES), F32)],
        scratch_shapes=[],
        args=(cache_t, q, kv_new, kv_new, kv_new_t, bias_rows),
        name="cache_stream",
        phases=nb * (1 + 2 * kv_heads))


def _sample_merge_body(x_ref, ng_ref, wg_ref, pa_ref, pb_ref, wo_ref, e_ref, sink_ref,
                       o1_ref, l1_ref, o2_ref, l2_ref, o3_ref, l3_ref, ob_ref, lb_ref, y_ref):
    def store(y):
        y_ref[...] = y

    for _ in _merge_math(store, x_ref[...], ng_ref, wg_ref, pa_ref, pb_ref, wo_ref, e_ref, sink_ref,
                         [o1_ref[...], o2_ref[...], o3_ref[...]],
                         [l1_ref[...], l2_ref[...], l3_ref[...]], ob_ref[...], lb_ref[...]):
        pass


def _sample_merge(x, ng, w_gates, p_a, p_b, w_o, expand, sink_row, o_list, ml_list):
    args = [x, ng, w_gates, p_a, p_b, w_o, expand, sink_row]
    for o, l in zip(o_list, ml_list):
        args += [o, l]
    return pl.pallas_call(
        _sample_merge_body,
        grid=(1,),
        in_specs=[_whole(a.shape) for a in args],
        out_specs=_whole(x.shape),
        out_shape=jax.ShapeDtypeStruct(x.shape, F32),
        compiler_params=_params(dimension_semantics=("arbitrary",)),
        name="sample_merge",
    )(*args)


def _buffer_bias_rows(window, dilation):
    r = np.arange(window)
    steps = (window - r) // dilation
    slopes = np.exp2(-np.arange(1, N_HEADS + 1, dtype=np.float32))
    bias = -slopes[:, None] * steps[None].astype(np.float32)
    return np.where((r % dilation == 0)[None], bias * LOG2E, np.float32(NEG)).astype(np.float32)


def _band_bias_table():
    q = np.arange(BLOCK)[:, None]
    kk = np.arange(2 * BLOCK)[None, :]
    delta = q + BLOCK - kk
    valid = (delta >= 0) & (delta <= BLOCK)
    slopes = np.exp2(-np.arange(1, N_HEADS + 1, dtype=np.float32))
    bias = -slopes[:, None, None] * delta[None].astype(np.float32)
    normal = np.where(valid[None], bias, np.float32(NEG))
    first = np.where((valid & (kk >= BLOCK))[None], bias, np.float32(NEG))
    return (np.stack([first, normal]) * LOG2E).astype(np.float32)


def _block_diag_ones():
    i = np.arange(MXU_WIDTH)
    return (i[:, None] // HEAD_DIM == i[None, :] // HEAD_DIM).astype(np.float32)


def _head_expand_matrix():
    e = np.zeros((LANES, A_WIDTH), np.float32)
    for pos in range(N_HEADS):
        e[_stat_lane(pos), pos * HEAD_DIM:(pos + 1) * HEAD_DIM] = 1.0
    return e


def _b_column_order():
    return np.concatenate([np.arange(HEAD_DIM) + HEAD_DIM * hd for hd in B_HEAD_ORDER])


def _prepare_weights(norm_gain, w_in, qk_norm_a, qk_norm_b, b_sinks, w_branch_a, w_branch_b, w_out):
    w = w_in[0].astype(BF16)
    order = _b_column_order()

    def reorder_heads(cols):
        half = N_HEADS // 2
        return cols.reshape(-1, 2, half, HEAD_DIM).swapaxes(1, 2).reshape(cols.shape)

    assert B_HEAD_ORDER == tuple(h for pair in zip(range(4), range(4, 8)) for h in pair)
    b_q0, z_b0 = 9 * A_WIDTH, QKV_WIDTH + A_WIDTH
    w = lax.dynamic_update_slice(w, reorder_heads(w[:, b_q0:b_q0 + A_WIDTH]), (0, b_q0))
    w = lax.dynamic_update_slice(w, reorder_heads(w[:, z_b0:z_b0 + A_WIDTH]), (0, z_b0))
    w_qkv, w_gates = w, w[:, QKV_WIDTH:]
    scale = HEAD_DIM ** -0.5 * LOG2E
    rows = []
    for g in range(3):
        rows.append(jnp.tile(qk_norm_a[0, g, 0], N_HEADS) * scale)
        rows.append(jnp.tile(qk_norm_a[0, g, 1], N_HEADS))
    rows.append(jnp.tile(qk_norm_b[0, 0], N_HEADS) * scale)
    rows.append(jnp.tile(qk_norm_b[0, 1], N_HEADS))
    gains = jnp.stack(rows).astype(F32)
    lanes = np.array([_stat_lane(pos) for pos in range(N_HEADS)])
    sink_row = jnp.zeros((1, LANES), F32).at[0, lanes].set(b_sinks[0][np.array(B_HEAD_ORDER)].astype(F32) * LOG2E)
    return dict(ng=norm_gain.astype(F32), w_qkv=w_qkv, w_gates=w_gates, gains=gains, sink_row=sink_row,
                p_a=w_branch_a[0].astype(BF16), p_b=w_branch_b[0][order].astype(BF16),
                w_o=w_out[0].astype(BF16))


def _cache_out(t):
    nb, two, width, rows = t.shape
    t = t.reshape(nb, two, width // HEAD_DIM, HEAD_DIM, rows)
    return jnp.transpose(t, (0, 4, 1, 2, 3))[None]


STREAM_BATCH = (8, 2, 1, 16)


def _layer(x_prompt, x_sample, caches, wts, consts):
    nb, seq, _ = x_prompt.shape
    dec = x_sample.shape[0]
    x2 = x_sample.reshape(dec, D_MODEL)
    q_s, kv_a, kv_b, kvt_a, kvt_b = _sample_proj(x2, wts["ng"], wts["w_qkv"], consts["bd"], wts["gains"])

    def stream_job(mixer):
        cache_t = jnp.transpose(caches[mixer][0], (0, 2, 3, 4, 1))
        kv_new, kv_new_t = (kv_b, kvt_b) if mixer == 3 else (kv_a[mixer], kvt_a[mixer])
        return _cache_stream_job(cache_t, q_s[:, None, :], mixer, kv_new[:, :, None, :], kv_new_t,
                                 consts["rows"][mixer], STREAM_BATCH[mixer])

    def attention_job(g):
        d = A_DILATIONS[g]
        shape = (nb * d, seq // d, A_WIDTH)
        return _attention_job(*(a.reshape(shape) for a in qkv[3 * g:3 * g + 3]), consts["bias"], NATURAL_ORDER)

    qkv, a3 = _run_pair(
        _prompt_proj_job(x_prompt, wts["ng"], wts["w_qkv"], consts["bd"], wts["gains"]), stream_job(2),
        interleave=False)
    prompt_caches = [_cache_out(c) for c in
                     _prompt_cache(x_prompt, wts["ng"], wts["w_qkv"], consts["bd"], wts["gains"])]
    attn = [None] * 3
    attn[0] = _run(attention_job(0))
    attn[1] = _run(attention_job(1))
    attn[2] = _run(attention_job(2))
    a1 = _run(stream_job(0))
    (o_b, ml_b), b = _run_pair(
        _attention_job(qkv[9], qkv[10], qkv[11], consts["bias"], B_HEAD_ORDER), stream_job(3),
        interleave=False)
    o_list, ml_list = [a[0] for a in attn], [a[1] for a in attn]
    (y_prompt,), a2 = _run_pair(
        _prompt_merge_job(x_prompt, wts["ng"], wts["w_gates"], wts["p_a"], wts["p_b"], wts["w_o"],
                          consts["expand"], wts["sink_row"], o_list, ml_list, o_b, ml_b), stream_job(1),
        interleave=True)

    streams = (a1, a2, a3, b)
    rows = lambda a: a.reshape(a.shape[0], a.shape[2])
    sample_caches = [jnp.transpose(s[0], (0, 4, 1, 2, 3))[None] for s in streams]
    y_sample = _sample_merge(x2, wts["ng"], wts["w_gates"], wts["p_a"], wts["p_b"], wts["w_o"],
                             consts["expand"], wts["sink_row"],
                             [rows(s[1]) for s in streams], [rows(s[2]) for s in streams])
    return y_prompt, y_sample.reshape(dec, 1, D_MODEL), prompt_caches, sample_caches


def _constants():
    assert all(B_HEAD_ORDER[pos] // (N_HEADS // B_KV_HEADS) == pos % B_KV_HEADS for pos in range(N_HEADS))
    rows = [jnp.asarray(_buffer_bias_rows(w, d)) for w, d in zip(A_WINDOWS, A_DILATIONS)]
    rows.append(jnp.asarray(_buffer_bias_rows(B_WINDOW, 1)[np.array(B_HEAD_ORDER)]))
    return dict(bias=jnp.asarray(_band_bias_table()), bd=jnp.asarray(_block_diag_ones(), BF16),
                expand=jnp.asarray(_head_expand_matrix(), BF16), rows=rows)


def kernel(x_prompt, x_sample, cache_a1_kv, cache_a2_kv, cache_a3_kv, cache_b_kv, norm_gain, w_in,
           qk_norm_a, qk_norm_b, b_sinks, w_branch_a, w_branch_b, w_out):
    wts = _prepare_weights(norm_gain, w_in, qk_norm_a, qk_norm_b, b_sinks, w_branch_a, w_branch_b, w_out)
    consts = _constants()
    y_prompt, y_sample, pc, sc = _layer(x_prompt, x_sample, (cache_a1_kv, cache_a2_kv, cache_a3_kv, cache_b_kv),
                                        wts, consts)
    return (y_prompt, y_sample, pc[0], pc[1], pc[2], pc[3], sc[0], sc[1], sc[2], sc[3])
```
